```python
import math
import jax
import jax.numpy as jnp
from jax import lax
import numpy as np

D_MODEL = 1024
BATCH = 16
SEQ = 4096
DEPTH = 4
DEC_BATCH = 32
DEC_SEQ = 16
PAST_LEN = 2048

CHUNK = 64
N_MIXERS = 3
N_A_LAYERS = (DEPTH + 2) // 3
N_B_LAYERS = (DEPTH + 1) // 3
N_C_LAYERS = DEPTH // 3

A_HEADS = 16
A_KV_HEADS = 4
A_HD = 64
A_GROUP = A_HEADS // A_KV_HEADS
WINDOW = 128
WIN_CHUNKS = WINDOW // CHUNK
A_QKV = (A_HEADS + 2 * A_KV_HEADS) * A_HD

B_HD = 64
B_HEADS = D_MODEL // B_HD
DECAY_LORA = 64
AAA_LORA = 64
GATE_LORA = 128
GN_EPS = 64e-5

C_HD = 64
C_HEADS = D_MODEL // (2 * C_HD)
C_KD = 2 * C_HD
C_VD = 2 * C_HD
Q_BLOCK = 128

N_GROUPS = 4
EXPERTS_PER_GROUP = 8
N_EXPERTS = N_GROUPS * EXPERTS_PER_GROUP
TOP_K = 2
D_EXPERT = 256
MOE_BLOCK = 128

ALPHA = (2 * DEPTH) ** 0.25
BETA = (8 * DEPTH) ** -0.25
LN_EPS = 1e-5
NEG = -1e30

kernel_name = 'hybrid_streaming_encoder_step'

F32 = jnp.float32


def layer_norm(x, g, b):
    xf = x.astype(F32)
    mu = jnp.mean(xf, -1, keepdims=True)
    var = jnp.mean(jnp.square(xf - mu), -1, keepdims=True)
    return ((xf - mu) * lax.rsqrt(var + LN_EPS) * g + b).astype(x.dtype)


def ada_mod(c, w, b):
    m = jax.nn.silu(c) @ w + b
    return jnp.split(m[:, None, :], 6, axis=-1)


def modulate(x, shift, scale):
    return x * (1 + scale) + shift


def sink_softmax(s, sink):
    m = jnp.maximum(jnp.max(s, -1, keepdims=True), sink)
    p = jnp.exp(s - m)
    return p / (jnp.sum(p, -1, keepdims=True) + jnp.exp(sink - m))


def swa_split(h, w_qkv):
    return jnp.split(h @ w_qkv, [A_HEADS * A_HD, (A_HEADS + A_KV_HEADS) * A_HD], axis=-1)


def swa_prompt(h, w_qkv, sinks, w_o):
    B, T, _ = h.shape
    nc = T // CHUNK
    q, k, v = swa_split(h, w_qkv)
    q = q.reshape(B, nc, CHUNK, A_KV_HEADS, A_GROUP, A_HD)
    k = k.reshape(B, T, A_KV_HEADS, A_HD)
    v = v.reshape(B, T, A_KV_HEADS, A_HD)

    def band(z):
        zp = jnp.pad(z, ((0, 0), (WINDOW, 0), (0, 0), (0, 0))).reshape(B, nc + WIN_CHUNKS, CHUNK, A_KV_HEADS, A_HD)
        return jnp.concatenate([zp[:, j:j + nc] for j in range(WIN_CHUNKS + 1)], axis=2)

    kb, vb = band(k), band(v)
    key_chunk = jnp.arange(nc)[:, None] - WIN_CHUNKS + jnp.arange((WIN_CHUNKS + 1) * CHUNK)[None, :] // CHUNK
    valid = (key_chunk >= 0)[None, :, None, None, None, :]
    s = jnp.einsum('bnqkgd,bnskd->bnkgqs', q, kb).astype(F32) * (A_HD ** -0.5)
    s = jnp.where(valid, s, NEG)
    sink = sinks.astype(F32).reshape(A_KV_HEADS, A_GROUP)[None, None, :, :, None, None]
    p = sink_softmax(s, sink).astype(vb.dtype)
    o = jnp.einsum('bnkgqs,bnskd->bnqkgd', p, vb).reshape(B, T, A_HEADS * A_HD)
    return o @ w_o, k[:, T - WINDOW:], v[:, T - WINDOW:]


def swa_sample(h, cache_k, cache_v, w_qkv, sinks, w_o):
    B, S, _ = h.shape
    q, k, v = swa_split(h, w_qkv)
    q = q.reshape(B, S, A_KV_HEADS, A_GROUP, A_HD)
    kk = jnp.concatenate([cache_k.astype(k.dtype), k.reshape(B, S, A_KV_HEADS, A_HD)], axis=1)
    vv = jnp.concatenate([cache_v.astype(v.dtype), v.reshape(B, S, A_KV_HEADS, A_HD)], axis=1)
    s = jnp.einsum('bqkgd,bskd->bkgqs', q, kk).astype(F32) * (A_HD ** -0.5)
    sink = sinks.astype(F32).reshape(A_KV_HEADS, A_GROUP)[None, :, :, None, None]
    p = sink_softmax(s, sink).astype(vv.dtype)
    o = jnp.einsum('bkgqs,bskd->bqkgd', p, vv).reshape(B, S, A_HEADS * A_HD)
    return o @ w_o, kk[:, S:], vv[:, S:]


def rwkv7(h, h_prev, S0, mu, w_rkv, w0, w1, w2, a0, a1, a2, g1, g2, k_k, k_a, r_k, lnx_g, lnx_b, w_o):
    B, T, D = h.shape
    xx = jnp.concatenate([h_prev[:, None, :].astype(h.dtype), h[:, :-1]], axis=1) - h
    xr, xw, xk, xv, xa, xg = (h + xx * mu[i] for i in range(6))
    r = xr @ w_rkv[0]
    k = xk @ w_rkv[1]
    v = xv @ w_rkv[2]
    w = -jax.nn.softplus(-(w0 + jnp.tanh(xw @ w1) @ w2).astype(F32)) - 0.5
    decay = jnp.exp(-jnp.exp(w))
    a = jax.nn.sigmoid((a0 + (xa @ a1) @ a2).astype(F32))
    g = jax.nn.sigmoid(xg @ g1) @ g2

    def heads(z):
        return z.astype(F32).reshape(B, T, B_HEADS, B_HD)

    r, k, v, decay, a = heads(r), heads(k), heads(v), heads(decay), heads(a)
    kk = k * k_k.astype(F32).reshape(B_HEADS, B_HD)
    kk = kk / jnp.maximum(jnp.sqrt(jnp.sum(kk * kk, -1, keepdims=True)), 1e-12)
    k = k * (1 + (a - 1) * k_a.astype(F32).reshape(B_HEADS, B_HD))

    def step(S, inp):
        r_t, w_t, k_t, v_t, kk_t, a_t = inp
        sa = jnp.einsum('bhij,bhj->bhi', S, kk_t)
        S = S * w_t[:, :, None, :] - sa[..., None] * (kk_t * a_t)[:, :, None, :] + v_t[..., None] * k_t[:, :, None, :]
        return S, jnp.einsum('bhij,bhj->bhi', S, r_t)

    xs_tm = tuple(jnp.moveaxis(z, 1, 0) for z in (r, decay, k, v, kk, a))
    S_T, y = lax.scan(step, S0.astype(F32), xs_tm)
    y = jnp.moveaxis(y, 0, 1)
    ym = jnp.mean(y, -1, keepdims=True)
    yv = jnp.mean(jnp.square(y - ym), -1, keepdims=True)
    y = ((y - ym) * lax.rsqrt(yv + GN_EPS)).reshape(B, T, D) * lnx_g + lnx_b
    bonus = (jnp.sum(r * k * r_k.astype(F32), -1, keepdims=True) * v).reshape(B, T, D)
    out = ((y + bonus) * g.astype(F32)).astype(h.dtype) @ w_o
    return out, S_T.astype(S0.dtype), h[:, -1]


def diff_lambda(lam, lam_init):
    lam = lam.astype(F32)
    return jnp.exp(jnp.sum(lam[0] * lam[1])) - jnp.exp(jnp.sum(lam[2] * lam[3])) + lam_init


def diff_attend(q, k, v, lam_full, mask):
    s = jnp.einsum('bqhmd,bshmd->bhmqs', q, k).astype(F32) * (C_HD ** -0.5)
    if mask is not None:
        s = jnp.where(mask, s, NEG)
    p = jax.nn.softmax(s, axis=-1)
    attn = p[:, :, 0] - lam_full * p[:, :, 1]
    return jnp.einsum('bhqs,bshe->bqhe', attn.astype(v.dtype), v)


def diff_out(o, subln_g, lam_init, w_o):
    B, T = o.shape[:2]
    of = o.astype(F32)
    of = of * lax.rsqrt(jnp.mean(of * of, -1, keepdims=True) + LN_EPS) * subln_g * (1 - lam_init)
    return of.reshape(B, T, C_HEADS * C_VD).astype(o.dtype) @ w_o


def diff_split(h, w_qkv):
    B, T, _ = h.shape
    q, k, v = jnp.split(h @ w_qkv, 3, axis=-1)
    return (q.reshape(B, T, C_HEADS, 2, C_HD), k.reshape(B, T, C_HEADS, 2, C_HD), v.reshape(B, T, C_HEADS, C_VD))


def diff_prompt(h, w_qkv, lam, subln_g, w_o, lam_init):
    B, T, _ = h.shape
    q, k, v = diff_split(h, w_qkv)
    lam_full = diff_lambda(lam, lam_init)
    blocks = []
    for i in range(T // Q_BLOCK):
        lo, hi = i * Q_BLOCK, (i + 1) * Q_BLOCK
        q_chunk = jnp.arange(lo, hi) // CHUNK
        k_chunk = jnp.arange(hi) // CHUNK
        mask = (k_chunk[None, :] <= q_chunk[:, None])[None, None, None]
        blocks.append(diff_attend(q[:, lo:hi], k[:, :hi], v[:, :hi], lam_full, mask))
    o = jnp.concatenate(blocks, axis=1)
    return diff_out(o, subln_g, lam_init, w_o), k.reshape(B, T, C_HEADS, C_KD), v


def diff_sample(h, cache_k, cache_v, w_qkv, lam, subln_g, w_o, lam_init):
    B, S, _ = h.shape
    L = cache_k.shape[1]
    q, k, v = diff_split(h, w_qkv)
    lam_full = diff_lambda(lam, lam_init)
    kk = jnp.concatenate([cache_k.astype(k.dtype).reshape(B, L, C_HEADS, 2, C_HD), k], axis=1)
    vv = jnp.concatenate([cache_v.astype(v.dtype), v], axis=1)
    o = diff_attend(q, kk, vv, lam_full, None)
    return diff_out(o, subln_g, lam_init, w_o), k.reshape(B, S, C_HEADS, C_KD), v


def expert_dispatch(x, experts, gates, w1, w3, w2):
    N, D = x.shape
    A = N * TOP_K
    e_flat = experts.reshape(-1).astype(jnp.int32)
    g_flat = gates.reshape(-1)
    tok = jnp.repeat(jnp.arange(N, dtype=jnp.int32), TOP_K)
    order = jnp.argsort(e_flat)
    e_sorted = e_flat[order]
    counts = jnp.bincount(e_flat, length=N_EXPERTS)
    start = jnp.cumsum(counts) - counts
    padded = (counts + MOE_BLOCK - 1) // MOE_BLOCK * MOE_BLOCK
    pend = jnp.cumsum(padded)
    pstart = pend - padded
    dest = pstart[e_sorted] + (jnp.arange(A, dtype=jnp.int32) - start[e_sorted])
    n_blocks = -(-(A + N_EXPERTS * (MOE_BLOCK - 1)) // MOE_BLOCK)
    cap = n_blocks * MOE_BLOCK
    slot_tok = jnp.zeros((cap,), jnp.int32).at[dest].set(tok[order])
    slot_gate = jnp.zeros((cap,), g_flat.dtype).at[dest].set(g_flat[order])
    block_expert = jnp.minimum(jnp.searchsorted(pend, jnp.arange(n_blocks, dtype=pend.dtype) * MOE_BLOCK, side='right'), N_EXPERTS - 1)
    xb = x[slot_tok].reshape(n_blocks, MOE_BLOCK, D)

    def run_block(args):
        xb_i, e = args
        hdn = jax.nn.silu(xb_i @ w1[e]) * (xb_i @ w3[e])
        return hdn @ w2[e]

    yb = lax.map(run_block, (xb, block_expert)).reshape(cap, D)
    return jnp.zeros_like(x).at[slot_tok].add(yb * slot_gate[:, None].astype(x.dtype))


def hier_moe(x, w_grp, b_grp, w_rt, b_rt, w1, w3, w2):
    N = x.shape[0]
    gl = (x @ w_grp).astype(F32) + b_grp.astype(F32)
    gidx = jnp.argmax(gl, -1)
    gw = jnp.take_along_axis(jax.nn.softmax(gl, -1), gidx[:, None], -1)
    el = ((x @ w_rt).astype(F32) + b_rt.astype(F32)).reshape(N, N_GROUPS, EXPERTS_PER_GROUP)
    el = jnp.take_along_axis(el, jnp.broadcast_to(gidx[:, None, None], (N, 1, EXPERTS_PER_GROUP)), 1)[:, 0]
    tp, ti = lax.top_k(jax.nn.softmax(el, -1), TOP_K)
    gates = gw * tp / jnp.sum(tp, -1, keepdims=True)
    experts = gidx[:, None] * EXPERTS_PER_GROUP + ti
    return expert_dispatch(x, experts, gates, w1, w3, w2)


def setup_inputs(seed: int = 0) -> dict:
    key = jax.random.key(seed)
    ks = iter(jax.random.split(key, 64))
    D = D_MODEL

    def nrm(shape, std=1.0):
        return std * jax.random.normal(next(ks), shape, F32)

    return {
        'x_prompt': nrm((BATCH, SEQ, D)),
        'x_sample': nrm((DEC_BATCH, DEC_SEQ, D)),
        'c_prompt': nrm((BATCH, D)),
        'c_sample': nrm((DEC_BATCH, D)),
        'cache_k_l0': nrm((DEC_BATCH, WINDOW, A_KV_HEADS, A_HD)),
        'cache_v_l0': nrm((DEC_BATCH, WINDOW, A_KV_HEADS, A_HD)),
        'state_wkv_l1': nrm((DEC_BATCH, B_HEADS, B_HD, B_HD), 0.5),
        'state_shift_l1': nrm((DEC_BATCH, D)),
        'cache_k_l2': nrm((DEC_BATCH, PAST_LEN, C_HEADS, C_KD)),
        'cache_v_l2': nrm((DEC_BATCH, PAST_LEN, C_HEADS, C_VD)),
        'cache_k_l3': nrm((DEC_BATCH, WINDOW, A_KV_HEADS, A_HD)),
        'cache_v_l3': nrm((DEC_BATCH, WINDOW, A_KV_HEADS, A_HD)),
        'w_ada': nrm((DEPTH, D, 6 * D), 0.2 * D ** -0.5),
        'b_ada': nrm((DEPTH, 6 * D), 0.02),
        'ln_g': 1.0 + nrm((DEPTH, 2, D), 0.02),
        'ln_b': nrm((DEPTH, 2, D), 0.02),
        'a_w_qkv': nrm((N_A_LAYERS, D, A_QKV), D ** -0.5),
        'a_sinks': nrm((N_A_LAYERS, A_HEADS), 0.5),
        'a_w_o': nrm((N_A_LAYERS, A_HEADS * A_HD, D), BETA * (A_HEADS * A_HD) ** -0.5),
        'b_mu': jax.random.uniform(next(ks), (N_B_LAYERS, 6, D), F32),
        'b_w_rkv': nrm((N_B_LAYERS, 3, D, D), D ** -0.5),
        'b_w0': nrm((N_B_LAYERS, D), 0.5),
        'b_w1': nrm((N_B_LAYERS, D, DECAY_LORA), D ** -0.5),
        'b_w2': nrm((N_B_LAYERS, DECAY_LORA, D), 0.1 * DECAY_LORA ** -0.5),
        'b_a0': nrm((N_B_LAYERS, D), 0.5),
        'b_a1': nrm((N_B_LAYERS, D, AAA_LORA), D ** -0.5),
        'b_a2': nrm((N_B_LAYERS, AAA_LORA, D), 0.1 * AAA_LORA ** -0.5),
        'b_g1': nrm((N_B_LAYERS, D, GATE_LORA), D ** -0.5),
        'b_g2': nrm((N_B_LAYERS, GATE_LORA, D), GATE_LORA ** -0.5),
        'b_k_k': 0.85 + nrm((N_B_LAYERS, D), 0.05),
        'b_k_a': 1.0 + nrm((N_B_LAYERS, D), 0.05),
        'b_r_k': nrm((N_B_LAYERS, B_HEADS, B_HD), 0.1),
        'b_lnx_g': 1.0 + nrm((N_B_LAYERS, D), 0.02),
        'b_lnx_b': nrm((N_B_LAYERS, D), 0.02),
        'b_w_o': nrm((N_B_LAYERS, D, D), BETA * D ** -0.5),
        'c_w_qkv': nrm((N_C_LAYERS, D, 3 * D), D ** -0.5),
        'c_lam': nrm((N_C_LAYERS, 4, C_HD), 0.1),
        'c_subln_g': 1.0 + nrm((N_C_LAYERS, C_VD), 0.02),
        'c_w_o': nrm((N_C_LAYERS, C_HEADS * C_VD, D), BETA * D ** -0.5),
        'moe_w_grp': nrm((DEPTH, D, N_GROUPS), D ** -0.5),
        'moe_b_grp': nrm((DEPTH, N_GROUPS), 0.01),
        'moe_w_rt': nrm((DEPTH, D, N_EXPERTS), D ** -0.5),
        'moe_b_rt': nrm((DEPTH, N_EXPERTS), 0.01),
        'moe_w1': nrm((DEPTH, N_EXPERTS, D, D_EXPERT), D ** -0.5),
        'moe_w3': nrm((DEPTH, N_EXPERTS, D, D_EXPERT), D ** -0.5),
        'moe_w2': nrm((DEPTH, N_EXPERTS, D_EXPERT, D), BETA * D_EXPERT ** -0.5),
    }


def reference(x_prompt, x_sample, c_prompt, c_sample, cache_k_l0, cache_v_l0, state_wkv_l1, state_shift_l1,
              cache_k_l2, cache_v_l2, cache_k_l3, cache_v_l3, w_ada, b_ada, ln_g, ln_b,
              a_w_qkv, a_sinks, a_w_o, b_mu, b_w_rkv, b_w0, b_w1, b_w2, b_a0, b_a1, b_a2, b_g1, b_g2,
              b_k_k, b_k_a, b_r_k, b_lnx_g, b_lnx_b, b_w_o, c_w_qkv, c_lam, c_subln_g, c_w_o,
              moe_w_grp, moe_b_grp, moe_w_rt, moe_b_rt, moe_w1, moe_w3, moe_w2):
    xp, xs = x_prompt, x_sample
    a_caches = [(cache_k_l0, cache_v_l0), (cache_k_l3, cache_v_l3)]
    b_states = [(state_wkv_l1, state_shift_l1)]
    c_caches = [(cache_k_l2, cache_v_l2)]
    new_states = []
    for i in range(DEPTH):
        kind, j = i % N_MIXERS, i // N_MIXERS
        sh_p, sc_p, gt_p, shf_p, scf_p, gtf_p = ada_mod(c_prompt, w_ada[i], b_ada[i])
        sh_s, sc_s, gt_s, shf_s, scf_s, gtf_s = ada_mod(c_sample, w_ada[i], b_ada[i])
        hp = modulate(xp, sh_p, sc_p)
        hs = modulate(xs, sh_s, sc_s)
        if kind == 0:
            op, kp_, vp_ = swa_prompt(hp, a_w_qkv[j], a_sinks[j], a_w_o[j])
            os_, ks_, vs_ = swa_sample(hs, a_caches[j][0], a_caches[j][1], a_w_qkv[j], a_sinks[j], a_w_o[j])
            new_states.append((kp_, vp_, ks_, vs_))
        elif kind == 1:
            rw = (b_mu[j], b_w_rkv[j], b_w0[j], b_w1[j], b_w2[j], b_a0[j], b_a1[j], b_a2[j], b_g1[j], b_g2[j],
                  b_k_k[j], b_k_a[j], b_r_k[j], b_lnx_g[j], b_lnx_b[j], b_w_o[j])
            Bp = hp.shape[0]
            op, Sp, lastp = rwkv7(hp, jnp.zeros((Bp, D_MODEL), hp.dtype),
                                  jnp.zeros((Bp, B_HEADS, B_HD, B_HD), hp.dtype), *rw)
            os_, Ss, lasts = rwkv7(hs, b_states[j][1], b_states[j][0], *rw)
            new_states.append((Sp, lastp, Ss, lasts))
        else:
            lam_init = 0.8 - 0.6 * math.exp(-0.3 * i)
            op, kp_, vp_ = diff_prompt(hp, c_w_qkv[j], c_lam[j], c_subln_g[j], c_w_o[j], lam_init)
            os_, ks_, vs_ = diff_sample(hs, c_caches[j][0], c_caches[j][1], c_w_qkv[j], c_lam[j],
                                        c_subln_g[j], c_w_o[j], lam_init)
            new_states.append((kp_, vp_, ks_, vs_))
        xp = layer_norm(ALPHA * xp + (1 + gt_p) * op, ln_g[i, 0], ln_b[i, 0])
        xs = layer_norm(ALPHA * xs + (1 + gt_s) * os_, ln_g[i, 0], ln_b[i, 0])
        hp = modulate(xp, shf_p, scf_p)
        hs = modulate(xs, shf_s, scf_s)
        Bp, Tp = hp.shape[:2]
        Bs, Ts = hs.shape[:2]
        ff = hier_moe(jnp.concatenate([hp.reshape(-1, D_MODEL), hs.reshape(-1, D_MODEL)], axis=0),
                      moe_w_grp[i], moe_b_grp[i], moe_w_rt[i], moe_b_rt[i], moe_w1[i], moe_w3[i], moe_w2[i])
        fp = ff[:Bp * Tp].reshape(Bp, Tp, D_MODEL)
        fs = ff[Bp * Tp:].reshape(Bs, Ts, D_MODEL)
        xp = layer_norm(ALPHA * xp + (1 + gtf_p) * fp, ln_g[i, 1], ln_b[i, 1])
        xs = layer_norm(ALPHA * xs + (1 + gtf_s) * fs, ln_g[i, 1], ln_b[i, 1])
    (k0p, v0p, k0s, v0s), (wkv1p, sh1p, wkv1s, sh1s), (k2p, v2p, k2s, v2s), (k3p, v3p, k3s, v3s) = new_states
    return (xp, xs, k0p, v0p, k0s, v0s, wkv1p, sh1p, wkv1s, sh1s, k2p, v2p, k2s, v2s, k3p, v3p, k3s, v3s)
```

```python
import functools
import math

import jax
import jax.numpy as jnp
from jax import lax
from jax.experimental import pallas as pl
from jax.experimental.pallas import tpu as pltpu

F32 = jnp.float32
BF16 = jnp.bfloat16
I32 = jnp.int32

D = 1024
DEPTH = 4
LANES = 128
HD = 64

A_HEADS, A_KV = 16, 4
WINDOW, CHUNK = 128, 64
N_EXPERTS, N_GROUPS, EPG, D_EXPERT = 32, 4, 8, 256
GN_EPS = 64e-5
LN_EPS = 1e-5
NEG = -1e30
ALPHA = (2 * DEPTH) ** 0.25

FFN_BLK = 256
CMB_TM = 256
VMEM_LIMIT = 56 * 1024 * 1024


def _cp(sem):
    return pltpu.CompilerParams(dimension_semantics=sem, vmem_limit_bytes=VMEM_LIMIT)


def _dot(a, b):
    return jnp.dot(a, b, preferred_element_type=F32)


def _dot_nt(a, b):
    return lax.dot_general(a, b, (((1,), (1,)), ((), ())), preferred_element_type=F32)


def _split2(x):
    hi = x.astype(BF16)
    lo = (x - hi.astype(F32)).astype(BF16)
    return hi, lo


def _split3(x):
    hi = x.astype(BF16)
    r1 = x - hi.astype(F32)
    mid = r1.astype(BF16)
    lo = (r1 - mid.astype(F32)).astype(BF16)
    return hi, mid, lo


def _dot3(a, b, dot=_dot):
    ah, al = _split2(a)
    bh, bl = _split2(b)
    return dot(ah, bh) + (dot(ah, bl) + dot(al, bh))


def _dot_sel(a, sel_bf16):
    h, m, l = _split3(a)
    return _dot(h, sel_bf16) + (_dot(m, sel_bf16) + _dot(l, sel_bf16))


def _layer_norm(y, g, b):
    mu = jnp.mean(y, -1, keepdims=True)
    yc = y - mu
    var = jnp.mean(yc * yc, -1, keepdims=True)
    return yc * lax.rsqrt(var + LN_EPS) * g + b


def _mod_spec(per_token, tm, tiles_per_seq):
    if per_token:
        return pl.BlockSpec((tm, D), lambda i: (i, 0))
    return pl.BlockSpec((None, 1, D), lambda i: (i // tiles_per_seq, 0, 0))


def _full(shape):
    return pl.BlockSpec(shape, lambda *_: (0,) * len(shape))


def _ada_kernel(c_ref, w_ref, b_ref, o_ref):
    c = c_ref[...]
    s = c * jax.nn.sigmoid(c)
    o_ref[...] = _dot3(s, w_ref[...]) + b_ref[...]


def _ada_all(c_all, w_ada, b_ada):
    nb = c_all.shape[0]
    tn = 1536
    return pl.pallas_call(
        _ada_kernel,
        grid=(DEPTH, 6 * D // tn),
        in_specs=[pl.BlockSpec((nb, D), lambda l, j: (0, 0)),
                  pl.BlockSpec((None, D, tn), lambda l, j: (l, 0, j)),
                  pl.BlockSpec((None, 1, tn), lambda l, j: (l, 0, j))],
        out_specs=pl.BlockSpec((None, nb, tn), lambda l, j: (l, 0, j)),
        out_shape=jax.ShapeDtypeStruct((DEPTH, nb, 6 * D), F32),
        compiler_params=_cp(("arbitrary", "arbitrary")),
        name="ada",
    )(c_all, w_ada, b_ada.reshape(DEPTH, 1, 6 * D))


def _qkv_kernel(x_ref, sh_ref, sc_ref, w_ref, *o_refs, splits):
    h = (x_ref[...] * (1.0 + sc_ref[...]) + sh_ref[...]).astype(BF16)
    for o_ref, (c0, c1) in zip(o_refs, splits):
        o_ref[...] = _dot(h, w_ref[:, c0:c1]).astype(o_ref.dtype)


def _qkv(x, sh, sc, w_bf16, splits, dtypes, per_token, tm, tiles_per_seq):
    n = x.shape[0]
    nout = w_bf16.shape[1]
    mod = _mod_spec(per_token, tm, tiles_per_seq)
    return pl.pallas_call(
        functools.partial(_qkv_kernel, splits=splits),
        grid=(n // tm,),
        in_specs=[pl.BlockSpec((tm, D), lambda i: (i, 0)), mod, mod, _full((D, nout))],
        out_specs=[pl.BlockSpec((tm, c1 - c0), lambda i: (i, 0)) for c0, c1 in splits],
        out_shape=[jax.ShapeDtypeStruct((n, c1 - c0), dt) for (c0, c1), dt in zip(splits, dtypes)],
        compiler_params=_cp(("arbitrary",)),
        name="qkv",
    )(x, sh, sc, w_bf16)


def _gqa_rows(q_slab, k_slabs, v_slabs, sinks_ref, n_invalid):
    outs = []
    for p in range(A_HEADS // 2):
        kvh = p // 2
        j, r = kvh // 2, kvh % 2
        qp = q_slab(p)
        nq = qp.shape[0]
        half = lax.broadcasted_iota(I32, (nq, LANES), 1) // HD
        res = []
        for e in range(2):
            qm = jnp.where(half == e, qp, jnp.zeros_like(qp))
            kx = k_slabs[j][0] if e == r else k_slabs[j][1]
            s = _dot_nt(qm, kx) * (HD ** -0.5)
            if n_invalid is not None:
                col = lax.broadcasted_iota(I32, s.shape, 1)
                s = jnp.where(col >= n_invalid, s, NEG)
            sink = sinks_ref[2 * p + e]
            m = jnp.maximum(jnp.max(s, -1, keepdims=True), sink)
            pe = jnp.exp(s - m)
            den = jnp.sum(pe, -1, keepdims=True) + jnp.exp(sink - m)
            pv = _dot(pe.astype(BF16), v_slabs[j]) / den
            res.append(pv if e == r else pltpu.roll(pv, HD, 1))
        outs.append(jnp.where(half == 0, res[0], res[1]))
    return outs


def _kv_slabs(k_f32, v_f32):
    ks, vs = [], []
    for j in range(A_KV // 2):
        kj = k_f32[:, j * LANES:(j + 1) * LANES]
        ks.append((kj.astype(BF16), pltpu.roll(kj, HD, 1).astype(BF16)))
        vs.append(v_f32[:, j * LANES:(j + 1) * LANES].astype(BF16))
    return ks, vs


def _swa_prompt_kernel(sinks_ref, q_ref, kvp_ref, kvc_ref, o_ref, *, tq):
    i = pl.program_id(1)
    kv = jnp.concatenate([kvp_ref[...], kvc_ref[...]], axis=0)
    kw = A_KV * HD
    ks, vs = _kv_slabs(kv[:, :kw], kv[:, kw:])
    for c in range(tq // CHUNK):
        r0, r1 = c * CHUNK, c * CHUNK + WINDOW + CHUNK
        kc = [(a[r0:r1], b[r0:r1]) for a, b in ks]
        vc = [a[r0:r1] for a in vs]
        n_bad = max(WINDOW - c * CHUNK, 0)
        n_invalid = jnp.where(i == 0, n_bad, 0) if n_bad else None
        outs = _gqa_rows(lambda p: q_ref[c * CHUNK:(c + 1) * CHUNK, p * LANES:(p + 1) * LANES],
                         kc, vc, sinks_ref, n_invalid)
        for p, o in enumerate(outs):
            o_ref[c * CHUNK:(c + 1) * CHUNK, p * LANES:(p + 1) * LANES] = o.astype(o_ref.dtype)


def _swa_prompt(q, kv, sinks, nb, t):
    tq = min(256, t)
    tps = t // tq
    wb = tq // WINDOW
    return pl.pallas_call(
        functools.partial(_swa_prompt_kernel, tq=tq),
        grid=(nb, tps),
        in_specs=[pl.BlockSpec(memory_space=pltpu.SMEM),
                  pl.BlockSpec((tq, D), lambda b, i: (b * tps + i, 0)),
                  pl.BlockSpec((WINDOW, 2 * A_KV * HD), lambda b, i: (jnp.maximum((b * tps + i) * wb - 1, 0), 0)),
                  pl.BlockSpec((tq, 2 * A_KV * HD), lambda b, i: (b * tps + i, 0))],
        out_specs=pl.BlockSpec((tq, D), lambda b, i: (b * tps + i, 0)),
        out_shape=jax.ShapeDtypeStruct((nb * t, D), BF16),
        compiler_params=_cp(("arbitrary", "arbitrary")),
        name="swa_prompt",
    )(sinks, q, kv, kv)


def _swa_sample_kernel(sinks_ref, q_ref, ck_ref, cv_ref, kvn_ref, o_ref):
    kw = A_KV * HD
    kvn = kvn_ref[...]
    k_all = jnp.concatenate([ck_ref[...], kvn[:, :kw]], axis=0)
    v_all = jnp.concatenate([cv_ref[...], kvn[:, kw:]], axis=0)
    ks, vs = _kv_slabs(k_all, v_all)
    outs = _gqa_rows(lambda p: q_ref[:, p * LANES:(p + 1) * LANES], ks, vs, sinks_ref, None)
    for p, o in enumerate(outs):
        o_ref[:, p * LANES:(p + 1) * LANES] = o.astype(o_ref.dtype)


def _swa_sample(q, kv, cache_k, cache_v, sinks, nb, s):
    kw = A_KV * HD
    return pl.pallas_call(
        _swa_sample_kernel,
        grid=(nb,),
        in_specs=[pl.BlockSpec(memory_space=pltpu.SMEM),
                  pl.BlockSpec((s, D), lambda b: (b, 0)),
                  pl.BlockSpec((None, WINDOW, kw), lambda b: (b, 0, 0)),
                  pl.BlockSpec((None, WINDOW, kw), lambda b: (b, 0, 0)),
                  pl.BlockSpec((s, 2 * kw), lambda b: (b, 0))],
        out_specs=pl.BlockSpec((s, D), lambda b: (b, 0)),
        out_shape=jax.ShapeDtypeStruct((nb * s, D), BF16),
        compiler_params=_cp(("arbitrary",)),
        name="swa_sample",
    )(sinks, q, cache_k.reshape(nb, WINDOW, kw), cache_v.reshape(nb, WINDOW, kw), kv)


def _diff_lambda(lam_ref, lam_init):
    l = lam_ref[...]
    return (jnp.exp(jnp.sum(l[0:1] * l[1:2], -1, keepdims=True))
            - jnp.exp(jnp.sum(l[2:3] * l[3:4], -1, keepdims=True)) + lam_init)


def _stack_halves(q):
    half = lax.broadcasted_iota(I32, q.shape, 1) // HD
    z = jnp.zeros_like(q)
    return jnp.concatenate([jnp.where(half == 0, q, z), jnp.where(half == 1, q, z)], axis=0)


def _diff_finish(o2, n, lam_full, sg, lam_init):
    o = o2[:n] - lam_full * o2[n:]
    return o * lax.rsqrt(jnp.mean(o * o, -1, keepdims=True) + LN_EPS) * sg * (1.0 - lam_init)


def _diff_prompt_kernel(lam_ref, q_ref, k_ref, v_ref, sg_ref, o_ref, kb, vb, *, tq, lam_init):
    i = pl.program_id(2)

    @pl.when(i == 0)
    def _():
        kb[...] = k_ref[...].astype(BF16)
        vb[...] = v_ref[...].astype(BF16)

    qs = _stack_halves(q_ref[...])

    def step(j, carry, masked):
        m, l, acc = carry
        off = pl.multiple_of(j * tq, tq)
        s = _dot_nt(qs, kb[pl.ds(off, tq), :]) * (HD ** -0.5)
        if masked:
            row = lax.broadcasted_iota(I32, s.shape, 0) % tq
            col = lax.broadcasted_iota(I32, s.shape, 1)
            s = jnp.where(col // CHUNK <= row // CHUNK, s, NEG)
        m_new = jnp.maximum(m, jnp.max(s, -1, keepdims=True))
        a = jnp.exp(m - m_new)
        p = jnp.exp(s - m_new)
        l = a * l + jnp.sum(p, -1, keepdims=True)
        acc = a * acc + _dot(p.astype(BF16), vb[pl.ds(off, tq), :])
        return m_new, l, acc

    carry = (jnp.full((2 * tq, 1), NEG, F32), jnp.zeros((2 * tq, 1), F32), jnp.zeros((2 * tq, LANES), F32))
    carry = lax.fori_loop(0, i, lambda j, c: step(j, c, False), carry)
    _, l, acc = step(i, carry, True)
    o = _diff_finish(acc / l, tq, _diff_lambda(lam_ref, lam_init), sg_ref[...], lam_init)
    o_ref[...] = o.astype(o_ref.dtype)


def _diff_prompt(q, k, v, lam, subln_g, nb, t, lam_init):
    tq = min(256, t)
    nh = D // LANES
    q3, k3, v3 = (z.reshape(nb, t, D) for z in (q, k, v))
    return pl.pallas_call(
        functools.partial(_diff_prompt_kernel, tq=tq, lam_init=lam_init),
        grid=(nb, nh, t // tq),
        in_specs=[_full((4, HD)),
                  pl.BlockSpec((None, tq, LANES), lambda b, h, i: (b, i, h)),
                  pl.BlockSpec((None, t, LANES), lambda b, h, i: (b, 0, h)),
                  pl.BlockSpec((None, t, LANES), lambda b, h, i: (b, 0, h)),
                  _full((1, LANES))],
        out_specs=pl.BlockSpec((None, tq, LANES), lambda b, h, i: (b, i, h)),
        out_shape=jax.ShapeDtypeStruct((nb, t, D), BF16),
        scratch_shapes=[pltpu.VMEM((t, LANES), BF16), pltpu.VMEM((t, LANES), BF16)],
        compiler_params=_cp(("arbitrary", "arbitrary", "arbitrary")),
        name="diff_prompt",
    )(lam, q3, k3, v3, subln_g.reshape(1, LANES)).reshape(nb * t, D)


def _diff_sample_kernel(lam_ref, q_ref, kn_ref, vn_ref, ck_ref, cv_ref, sg_ref, o_ref, *, lam_init):
    n = q_ref.shape[0]
    qs = _stack_halves(q_ref[...])
    s1 = _dot_nt(qs, ck_ref[...].astype(BF16)) * (HD ** -0.5)
    s2 = _dot_nt(qs, kn_ref[...].astype(BF16)) * (HD ** -0.5)
    m = jnp.maximum(jnp.max(s1, -1, keepdims=True), jnp.max(s2, -1, keepdims=True))
    p1 = jnp.exp(s1 - m)
    p2 = jnp.exp(s2 - m)
    l = jnp.sum(p1, -1, keepdims=True) + jnp.sum(p2, -1, keepdims=True)
    acc = _dot(p1.astype(BF16), cv_ref[...].astype(BF16)) + _dot(p2.astype(BF16), vn_ref[...].astype(BF16))
    o = _diff_finish(acc / l, n, _diff_lambda(lam_ref, lam_init), sg_ref[...], lam_init)
    o_ref[...] = o.astype(o_ref.dtype)


def _diff_sample(q, k, v, cache_k, cache_v, lam, subln_g, nb, s, lam_init):
    nh = D // LANES
    past = cache_k.shape[1]
    q3, k3, v3 = (z.reshape(nb, s, D) for z in (q, k, v))
    new = pl.BlockSpec((None, s, LANES), lambda b, h: (b, 0, h))
    old = pl.BlockSpec((None, past, LANES), lambda b, h: (b, 0, h))
    return pl.pallas_call(
        functools.partial(_diff_sample_kernel, lam_init=lam_init),
        grid=(nb, nh),
        in_specs=[_full((4, HD)), new, new, new, old, old, _full((1, LANES))],
        out_specs=new,
        out_shape=jax.ShapeDtypeStruct((nb, s, D), BF16),
        compiler_params=_cp(("arbitrary", "arbitrary")),
        name="diff_sample",
    )(lam, q3, k3, v3, cache_k.reshape(nb, past, D), cache_v.reshape(nb, past, D),
      subln_g.reshape(1, LANES)).reshape(nb * s, D)


def _rwkv_proj_kernel(x_ref, xp_ref, hp_ref, sh_ref, sc_ref, mu_ref, wrkv_ref, w0_ref, w1_ref, w2_ref,
                      a0_ref, a1_ref, a2_ref, g1_ref, g2_ref,
                      r_ref, k_ref, v_ref, lw_ref, a_ref, g_ref, hl_ref):
    i = pl.program_id(1)
    sc1 = 1.0 + sc_ref[...]
    sh = sh_ref[...]
    h = x_ref[...] * sc1 + sh
    prev_in_seq = xp_ref[7:8, :] * sc1 + sh
    prev = jnp.where(i == 0, hp_ref[...], prev_in_seq)
    row = lax.broadcasted_iota(I32, h.shape, 0)
    hs = jnp.where(row == 0, prev, pltpu.roll(h, 1, 0))
    xx = hs - h
    mu = mu_ref[...]

    def mix(n):
        return (h + xx * mu[n:n + 1]).astype(BF16)

    r_ref[...] = _dot(mix(0), wrkv_ref[0])
    k_ref[...] = _dot(mix(2), wrkv_ref[1])
    v_ref[...] = _dot(mix(3), wrkv_ref[2])
    z = w0_ref[...] + _dot(jnp.tanh(_dot(mix(1), w1_ref[...])).astype(BF16), w2_ref[...])
    softplus_neg = jnp.maximum(-z, 0.0) + jnp.log(1.0 + jnp.exp(-jnp.abs(z)))
    lw_ref[...] = -jnp.exp(-softplus_neg - 0.5)
    a_ref[...] = jax.nn.sigmoid(a0_ref[...] + _dot(_dot(mix(4), a1_ref[...]).astype(BF16), a2_ref[...]))
    g_ref[...] = _dot(jax.nn.sigmoid(_dot(mix(5), g1_ref[...])).astype(BF16), g2_ref[...])
    hl_ref[...] = h[h.shape[0] - 1:, :]


def _rwkv_proj(x, h_prev, sh, sc, wts, nb, t):
    tm = min(512, t)
    tps = t // tm
    x3 = x.reshape(nb, t, D)
    tok = pl.BlockSpec((None, tm, D), lambda b, i: (b, i, 0))
    per_b = pl.BlockSpec((None, 1, D), lambda b, i: (b, 0, 0))
    ins = [tok,
           pl.BlockSpec((None, 8, D), lambda b, i: (b, jnp.maximum(i * (tm // 8) - 1, 0), 0)),
           per_b, per_b, per_b] + [_full(w.shape) for w in wts]
    outs = pl.pallas_call(
        _rwkv_proj_kernel,
        grid=(nb, tps),
        in_specs=ins,
        out_specs=[tok] * 6 + [per_b],
        out_shape=[jax.ShapeDtypeStruct((nb, t, D), F32)] * 6 + [jax.ShapeDtypeStruct((nb, 1, D), F32)],
        compiler_params=_cp(("arbitrary", "arbitrary")),
        name="rwkv_proj",
    )(x3, x3, h_prev.reshape(nb, 1, D), sh, sc, *wts)
    return outs


def _rwkv_chunk(r, k, v, lw, a, g, s_state, kk_w, ka_w, rk_w, gn_g, gn_b, bd, tri):
    L = r.shape[0]
    n2 = 2 * L
    lane_half = lax.broadcasted_iota(I32, (L, LANES), 1) // HD

    def seg(x):
        return _dot_sel(x, bd)

    def stack(x):
        z = jnp.zeros_like(x)
        return jnp.concatenate([jnp.where(lane_half == 0, x, z), jnp.where(lane_half == 1, x, z)], axis=0)

    kkr = k * kk_w
    kk = kkr / jnp.maximum(jnp.sqrt(seg(kkr * kkr)), 1e-12)
    kp = k * (1.0 + (a - 1.0) * ka_w)
    cum = _dot_sel_lhs(tri, lw)
    g_in = jnp.exp(cum)
    g_ex = jnp.exp(cum - lw)
    g_inv = jnp.exp(-cum)
    g_last = g_in[L - 1:L, :]
    at = kk * g_ex
    bt = kk * a * g_inv
    kt = kp * g_inv
    rt = r * g_in
    ar = jnp.concatenate([stack(at), stack(rt)], axis=0)
    bk = jnp.concatenate([stack(bt), stack(kt)], axis=0)
    vs = stack(v)
    pair = _dot3(ar, bk, _dot_nt)
    row = lax.broadcasted_iota(I32, (n2, n2), 0)
    col = lax.broadcasted_iota(I32, (n2, n2), 1)
    strict = col < row
    incl = col <= row
    zero = jnp.zeros((n2, n2), F32)
    x = jnp.where(strict, -pair[:n2, :n2], zero)
    tinv = jnp.where(row == col, 1.0, 0.0).astype(F32) + x
    for _ in range(int(math.log2(L)) - 1):
        x = _dot3(x, x)
        tinv = tinv + _dot3(tinv, x)
    ars = _dot3(ar, s_state, _dot_nt)
    rhs = ars[:n2] + _dot3(jnp.where(strict, pair[:n2, n2:], zero), vs)
    u = _dot3(tinv, rhs)
    y2 = ars[n2:] + _dot3(jnp.where(incl, -pair[n2:, :n2], zero), u) + _dot3(jnp.where(incl, pair[n2:, n2:], zero), vs)
    s_new = s_state * g_last + _dot3((-u).T, stack(bt * g_last)) + _dot3(vs.T, stack(kt * g_last))
    y = y2[:L] + y2[L:]
    ym = seg(y) * (1.0 / HD)
    yc = y - ym
    yv = seg(yc * yc) * (1.0 / HD)
    yn = yc * lax.rsqrt(yv + GN_EPS) * gn_g + gn_b
    bonus = seg(r * kp * rk_w) * v
    return ((yn + bonus) * g), s_new


def _dot_sel_lhs(sel_bf16, b):
    h, m, l = _split3(b)
    return _dot(sel_bf16, h) + (_dot(sel_bf16, m) + _dot(sel_bf16, l))


def _rwkv_scan_kernel(r_ref, k_ref, v_ref, lw_ref, a_ref, g_ref, s0_ref, hw_ref, y_ref, sT_ref, s_scr, *, L, nslab):
    i = pl.program_id(2)
    tt = r_ref.shape[0]

    @pl.when(i == 0)
    def _():
        s_scr[...] = s0_ref[...]

    rr = lax.broadcasted_iota(I32, (LANES, LANES), 0)
    cc = lax.broadcasted_iota(I32, (LANES, LANES), 1)
    bd = jnp.where(rr // HD == cc // HD, 1.0, 0.0).astype(BF16)
    rl = lax.broadcasted_iota(I32, (L, L), 0)
    cl = lax.broadcasted_iota(I32, (L, L), 1)
    tri = jnp.where(cl <= rl, 1.0, 0.0).astype(BF16)

    def chunk(c, carry):
        rows = pl.ds(pl.multiple_of(c * L, L), L)
        for s in range(nslab):
            lanes = slice(s * LANES, (s + 1) * LANES)
            hw = hw_ref[:, lanes]
            out, s_new = _rwkv_chunk(r_ref[rows, lanes], k_ref[rows, lanes], v_ref[rows, lanes],
                                     lw_ref[rows, lanes], a_ref[rows, lanes], g_ref[rows, lanes], s_scr[s],
                                     hw[0:1], hw[1:2], hw[2:3], hw[3:4], hw[4:5], bd, tri)
            y_ref[rows, lanes] = out.astype(y_ref.dtype)
            s_scr[s] = s_new
        return carry

    lax.fori_loop(0, tt // L, chunk, 0)

    @pl.when(i == pl.num_programs(2) - 1)
    def _():
        sT_ref[...] = s_scr[...]


def _rwkv_scan(r, k, v, lw, a, g, s0_bd, head_w, nb, t):
    L = min(64, t)
    tt = min(256, t)
    nslab = 4
    ngrp = D // (LANES * nslab)
    w = nslab * LANES
    tok = pl.BlockSpec((None, tt, w), lambda b, sg, i: (b, i, sg))
    st = pl.BlockSpec((None, nslab, LANES, LANES), lambda b, sg, i: (b, sg, 0, 0))
    y, s_t = pl.pallas_call(
        functools.partial(_rwkv_scan_kernel, L=L, nslab=nslab),
        grid=(nb, ngrp, t // tt),
        in_specs=[tok] * 6 + [st, pl.BlockSpec((8, w), lambda b, sg, i: (0, sg))],
        out_specs=[tok, st],
        out_shape=[jax.ShapeDtypeStruct((nb, t, D), BF16),
                   jax.ShapeDtypeStruct((nb, D // LANES, LANES, LANES), F32)],
        scratch_shapes=[pltpu.VMEM((nslab, LANES, LANES), F32)],
        compiler_params=_cp(("arbitrary", "arbitrary", "arbitrary")),
        name="rwkv_scan",
    )(r, k, v, lw, a, g, s0_bd, head_w)
    return y.reshape(nb * t, D), s_t


def _state_to_blockdiag(s):
    nb = s.shape[0]
    s = s.astype(F32).reshape(nb, D // LANES, 2, HD, HD)
    z = jnp.zeros_like(s[:, :, 0])
    top = jnp.concatenate([s[:, :, 0], z], axis=-1)
    bot = jnp.concatenate([z, s[:, :, 1]], axis=-1)
    return jnp.concatenate([top, bot], axis=-2)


def _blockdiag_to_state(sb):
    nb = sb.shape[0]
    s0 = sb[:, :, :HD, :HD]
    s1 = sb[:, :, HD:, HD:]
    return jnp.stack([s0, s1], axis=2).reshape(nb, D // HD, HD, HD)


def _route(h, wr_hl_ref, wr_hi_ref, br_ref):
    hh, hl = _split2(h)
    both = _dot(hh, wr_hl_ref[...])
    lg = both[:, :LANES] + both[:, LANES:] + _dot(hl, wr_hi_ref[...]) + br_ref[...]
    lane = lax.broadcasted_iota(I32, lg.shape, 1)
    lane_f = lane.astype(F32)
    big = 1e9
    is_g = (lane >= N_EXPERTS) & (lane < N_EXPERTS + N_GROUPS)
    gl = jnp.where(is_g, lg, -jnp.inf)
    gmax = jnp.max(gl, -1, keepdims=True)
    gidx = jnp.min(jnp.where(gl == gmax, lane_f - N_EXPERTS, big), -1, keepdims=True)
    gw = 1.0 / jnp.sum(jnp.exp(gl - gmax), -1, keepdims=True)
    in_grp = (lane < N_EXPERTS) & ((lane // EPG).astype(F32) == gidx)
    el = jnp.where(in_grp, lg, -jnp.inf)
    m1 = jnp.max(el, -1, keepdims=True)
    i1 = jnp.min(jnp.where(el == m1, lane_f, big), -1, keepdims=True)
    el2 = jnp.where(lane_f == i1, -jnp.inf, el)
    m2 = jnp.max(el2, -1, keepdims=True)
    i2 = jnp.min(jnp.where(el2 == m2, lane_f, big), -1, keepdims=True)
    e2 = jnp.exp(m2 - m1)
    g1 = gw / (1.0 + e2)
    g2 = gw * e2 / (1.0 + e2)
    return jnp.where(lane == 0, i1, jnp.where(lane == 1, i2, jnp.where(lane == 2, g1, jnp.where(lane == 3, g2, 0.0))))


def _outln_kernel(*refs, aliased):
    if aliased:
        refs = refs[2:]
    (o_ref, x_ref, gt_ref, w_ref, g_ref, b_ref, shf_ref, scf_ref, wr_hl_ref, wr_hi_ref, br_ref,
     xn_ref, hm_ref, rt_ref) = refs
    op = _dot(o_ref[...], w_ref[...])
    xn = _layer_norm(ALPHA * x_ref[...] + (1.0 + gt_ref[...]) * op, g_ref[...], b_ref[...])
    xn_ref[...] = xn
    h = xn * (1.0 + scf_ref[...]) + shf_ref[...]
    hm_ref[...] = h
    rt_ref[...] = _route(h, wr_hl_ref, wr_hi_ref, br_ref)


def _outln(o, x, gt, w_o, ln_g, ln_b, shf, scf, wr_hl, wr_hi, br, per_token, tm, tiles_per_seq, n_all, hm_rt=None):
    n = x.shape[0]
    nt = n // tm
    blk0 = 0 if hm_rt is None else (n_all - n) // tm
    steps = nt if hm_rt is not None else n_all // tm
    if per_token:
        mod = pl.BlockSpec((tm, D), lambda i: (jnp.minimum(i, nt - 1), 0))
    else:
        mod = pl.BlockSpec((None, 1, D), lambda i: (jnp.minimum(i, nt - 1) // tiles_per_seq, 0, 0))
    tok = pl.BlockSpec((tm, D), lambda i: (jnp.minimum(i, nt - 1), 0))
    ins = [tok, tok, mod, _full((D, D)), _full((1, D)), _full((1, D)), mod, mod,
           _full((D, 2 * LANES)), _full((D, LANES)), _full((1, LANES))]
    args = [o, x, gt, w_o, ln_g, ln_b, shf, scf, wr_hl, wr_hi, br]
    aliases = {}
    if hm_rt is not None:
        ins = [pl.BlockSpec(memory_space=pl.ANY)] * 2 + ins
        args = list(hm_rt) + args
        aliases = {0: 1, 1: 2}
    return pl.pallas_call(
        functools.partial(_outln_kernel, aliased=hm_rt is not None),
        grid=(steps,),
        in_specs=ins,
        out_specs=[tok, pl.BlockSpec((tm, D), lambda i: (i + blk0, 0)), pl.BlockSpec((tm, LANES), lambda i: (i + blk0, 0))],
        out_shape=[jax.ShapeDtypeStruct((n, D), F32), jax.ShapeDtypeStruct((n_all, D), F32),
                   jax.ShapeDtypeStruct((n_all, LANES), F32)],
        input_output_aliases=aliases,
        compiler_params=_cp(("arbitrary",)),
        name="outln",
    )(*args)


def _gather_rows(idx_ref, n, src_hbm, dst, sem, stride=1, offset=0):
    def body(r, c):
        t = idx_ref[0, r * stride + offset]
        pltpu.make_async_copy(src_hbm.at[pl.ds(t, 1)], dst.at[pl.ds(r, 1)], sem).start()
        return c
    lax.fori_loop(0, n, body, 0)


def _ffn_kernel(be_ref, tok_ref, tokn_ref, x_hbm, w1_ref, w3_ref, w2_ref, y_ref, buf, sem):
    del be_ref
    i = pl.program_id(0)
    n = pl.num_programs(0)
    slot = i % 2

    @pl.when(i == 0)
    def _():
        _gather_rows(tok_ref, FFN_BLK, x_hbm, buf.at[0], sem.at[0])

    pltpu.make_async_copy(x_hbm.at[pl.ds(0, FFN_BLK)], buf.at[slot], sem.at[slot]).wait()

    @pl.when(i + 1 < n)
    def _():
        _gather_rows(tokn_ref, FFN_BLK, x_hbm, buf.at[1 - slot], sem.at[1 - slot])

    xb = buf[slot].astype(BF16)
    a = _dot(xb, w1_ref[...])
    hdn = (a * jax.nn.sigmoid(a)) * _dot(xb, w3_ref[...])
    y_ref[...] = _dot(hdn.astype(BF16), w2_ref[...])


def _ffn(block_expert, slot_tok, hm, w1, w3, w2):
    nblk = block_expert.shape[0]
    tok3 = slot_tok.reshape(nblk, 1, FFN_BLK)
    grid_spec = pltpu.PrefetchScalarGridSpec(
        num_scalar_prefetch=1,
        grid=(nblk,),
        in_specs=[pl.BlockSpec((None, 1, FFN_BLK), lambda i, be: (i, 0, 0), memory_space=pltpu.SMEM),
                  pl.BlockSpec((None, 1, FFN_BLK), lambda i, be: (jnp.minimum(i + 1, nblk - 1), 0, 0),
                               memory_space=pltpu.SMEM),
                  pl.BlockSpec(memory_space=pl.ANY),
                  pl.BlockSpec((None, D, D_EXPERT), lambda i, be: (be[i], 0, 0)),
                  pl.BlockSpec((None, D, D_EXPERT), lambda i, be: (be[i], 0, 0)),
                  pl.BlockSpec((None, D_EXPERT, D), lambda i, be: (be[i], 0, 0))],
        out_specs=pl.BlockSpec((FFN_BLK, D), lambda i, be: (i, 0)),
        scratch_shapes=[pltpu.VMEM((2, FFN_BLK, D), F32), pltpu.SemaphoreType.DMA((2,))],
    )
    return pl.pallas_call(
        _ffn_kernel,
        grid_spec=grid_spec,
        out_shape=jax.ShapeDtypeStruct((nblk * FFN_BLK, D), F32),
        compiler_params=_cp(("arbitrary",)),
        name="moe_ffn",
    )(block_expert, tok3, tok3, hm, w1, w3, w2)


def _combine_kernel(pos_ref, posn_ref, yb_hbm, rt_ref, x_ref, gt_ref, g_ref, b_ref, o_ref, buf, sem):
    i = pl.program_id(0)
    n = pl.num_programs(0)
    slot = i % 2
    tm = x_ref.shape[0]

    def issue(idx_ref, s):
        _gather_rows(idx_ref, tm, yb_hbm, buf.at[s, 0], sem.at[s], stride=2, offset=0)
        _gather_rows(idx_ref, tm, yb_hbm, buf.at[s, 1], sem.at[s], stride=2, offset=1)

    @pl.when(i == 0)
    def _():
        issue(pos_ref, 0)

    for kk in range(2):
        pltpu.make_async_copy(yb_hbm.at[pl.ds(0, tm)], buf.at[slot, kk], sem.at[slot]).wait()

    @pl.when(i + 1 < n)
    def _():
        issue(posn_ref, 1 - slot)

    rt = rt_ref[...]
    f = rt[:, 2:3] * buf[slot, 0] + rt[:, 3:4] * buf[slot, 1]
    o_ref[...] = _layer_norm(ALPHA * x_ref[...] + (1.0 + gt_ref[...]) * f, g_ref[...], b_ref[...])


def _combine(pos, yb, rt, x, gt, ln_g, ln_b, per_token, tiles_per_seq):
    n = x.shape[0]
    tm = min(CMB_TM, n)
    nt = n // tm
    pos3 = pos.reshape(nt, 1, 2 * tm)
    mod = _mod_spec(per_token, tm, tiles_per_seq)
    tok = pl.BlockSpec((tm, D), lambda i: (i, 0))
    return pl.pallas_call(
        _combine_kernel,
        grid=(nt,),
        in_specs=[pl.BlockSpec((None, 1, 2 * tm), lambda i: (i, 0, 0), memory_space=pltpu.SMEM),
                  pl.BlockSpec((None, 1, 2 * tm), lambda i: (jnp.minimum(i + 1, nt - 1), 0, 0), memory_space=pltpu.SMEM),
                  pl.BlockSpec(memory_space=pl.ANY),
                  pl.BlockSpec((tm, LANES), lambda i: (i, 0)),
                  tok, mod, _full((1, D)), _full((1, D))],
        out_specs=tok,
        out_shape=jax.ShapeDtypeStruct((n, D), F32),
        scratch_shapes=[pltpu.VMEM((2, 2, tm, D), F32), pltpu.SemaphoreType.DMA((2,))],
        compiler_params=_cp(("arbitrary",)),
        name="moe_combine",
    )(pos3, pos3, yb, rt, x, gt, ln_g, ln_b)


def _dispatch(e_flat):
    n_assign = e_flat.shape[0]
    order = jnp.argsort(e_flat, stable=True).astype(I32)
    e_sorted = e_flat[order]
    counts = jnp.bincount(e_flat, length=N_EXPERTS).astype(I32)
    start = jnp.cumsum(counts) - counts
    padded = (counts + FFN_BLK - 1) // FFN_BLK * FFN_BLK
    pend = jnp.cumsum(padded)
    pstart = pend - padded
    dest = pstart[e_sorted] + (jnp.arange(n_assign, dtype=I32) - start[e_sorted])
    n_blocks = -(-(n_assign + N_EXPERTS * (FFN_BLK - 1)) // FFN_BLK)
    slot_tok = jnp.zeros((n_blocks * FFN_BLK,), I32).at[dest].set(order // 2)
    pos = jnp.zeros((n_assign,), I32).at[order].set(dest)
    block_expert = jnp.minimum(
        jnp.searchsorted(pend, jnp.arange(n_blocks, dtype=I32) * FFN_BLK, side='right'), N_EXPERTS - 1).astype(I32)
    return block_expert, slot_tok, pos


def kernel(x_prompt, x_sample, c_prompt, c_sample, cache_k_l0, cache_v_l0, state_wkv_l1, state_shift_l1, cache_k_l2, cache_v_l2, cache_k_l3, cache_v_l3, w_ada, b_ada, ln_g, ln_b, a_w_qkv, a_sinks, a_w_o, b_mu, b_w_rkv, b_w0, b_w1, b_w2, b_a0, b_a1, b_a2, b_g1, b_g2, b_k_k, b_k_a, b_r_k, b_lnx_g, b_lnx_b, b_w_o, c_w_qkv, c_lam, c_subln_g, c_w_o, moe_w_grp, moe_b_grp, moe_w_rt, moe_b_rt, moe_w1, moe_w3, moe_w2):
    nbp, tp, _ = x_prompt.shape
    nbs, ts, _ = x_sample.shape
    n_p, n_s = nbp * tp, nbs * ts
    n_all = n_p + n_s
    tm_p = min(512, tp)
    tps_p = tp // tm_p
    tm_s = n_s
    assert n_p % tm_s == 0 and n_p % CMB_TM == 0

    xp = x_prompt.reshape(n_p, D)
    xs = x_sample.reshape(n_s, D)
    mods = _ada_all(jnp.concatenate([c_prompt, c_sample], axis=0), w_ada, b_ada)

    a_caches = [(cache_k_l0, cache_v_l0), (cache_k_l3, cache_v_l3)]
    new_states = []
    for i in range(DEPTH):
        kind, j = i % 3, i // 3
        m_p = [mods[i, :nbp, n * D:(n + 1) * D].reshape(nbp, 1, D) for n in range(6)]
        m_sb = [mods[i, nbp:, n * D:(n + 1) * D].reshape(nbs, 1, D) for n in range(6)]
        m_s = [jnp.broadcast_to(m, (nbs, ts, D)).reshape(n_s, D) for m in m_sb]

        if kind == 0:
            w = a_w_qkv[j].astype(BF16)
            splits = ((0, D), (D, D + 2 * A_KV * HD))
            q_p, kv_p = _qkv(xp, m_p[0], m_p[1], w, splits, (BF16, F32), False, tm_p, tps_p)
            q_s, kv_s = _qkv(xs, m_s[0], m_s[1], w, splits, (BF16, F32), True, tm_s, 1)
            o_p = _swa_prompt(q_p, kv_p, a_sinks[j], nbp, tp)
            o_s = _swa_sample(q_s, kv_s, a_caches[j][0], a_caches[j][1], a_sinks[j], nbs, ts)
            kw = A_KV * HD
            kv_p3 = kv_p.reshape(nbp, tp, 2 * kw)[:, tp - WINDOW:]
            kv_s3 = kv_s.reshape(nbs, ts, 2 * kw)
            k_s = jnp.concatenate([a_caches[j][0], kv_s3[..., :kw].reshape(nbs, ts, A_KV, HD)], axis=1)[:, ts:]
            v_s = jnp.concatenate([a_caches[j][1], kv_s3[..., kw:].reshape(nbs, ts, A_KV, HD)], axis=1)[:, ts:]
            new_states.append((kv_p3[..., :kw].reshape(nbp, WINDOW, A_KV, HD),
                               kv_p3[..., kw:].reshape(nbp, WINDOW, A_KV, HD), k_s, v_s))
            w_o = a_w_o[j]
        elif kind == 1:
            wts = (b_mu[j], b_w_rkv[j].astype(BF16), b_w0[j].reshape(1, D), b_w1[j].astype(BF16),
                   b_w2[j].astype(BF16), b_a0[j].reshape(1, D), b_a1[j].astype(BF16), b_a2[j].astype(BF16),
                   b_g1[j].astype(BF16), b_g2[j].astype(BF16))
            zero8 = jnp.zeros((3, D), F32)
            head_w = jnp.concatenate([b_k_k[j].reshape(1, D), b_k_a[j].reshape(1, D), b_r_k[j].reshape(1, D),
                                      b_lnx_g[j].reshape(1, D), b_lnx_b[j].reshape(1, D), zero8], axis=0)
            *rk_p, hl_p = _rwkv_proj(xp, jnp.zeros((nbp, D), F32), m_p[0], m_p[1], wts, nbp, tp)
            *rk_s, hl_s = _rwkv_proj(xs, state_shift_l1, m_sb[0], m_sb[1], wts, nbs, ts)
            s0_p = jnp.zeros((nbp, D // LANES, LANES, LANES), F32)
            o_p, st_p = _rwkv_scan(*rk_p, s0_p, head_w, nbp, tp)
            o_s, st_s = _rwkv_scan(*rk_s, _state_to_blockdiag(state_wkv_l1), head_w, nbs, ts)
            new_states.append((_blockdiag_to_state(st_p), hl_p.reshape(nbp, D),
                               _blockdiag_to_state(st_s), hl_s.reshape(nbs, D)))
            w_o = b_w_o[j]
        else:
            lam_init = 0.8 - 0.6 * math.exp(-0.3 * i)
            w = c_w_qkv[j].astype(BF16)
            splits = ((0, D), (D, 2 * D), (2 * D, 3 * D))
            q_p, k_p, v_p = _qkv(xp, m_p[0], m_p[1], w, splits, (BF16, F32, F32), False, tm_p, tps_p)
            q_s, k_s, v_s = _qkv(xs, m_s[0], m_s[1], w, splits, (BF16, F32, F32), True, tm_s, 1)
            o_p = _diff_prompt(q_p, k_p, v_p, c_lam[j], c_subln_g[j], nbp, tp, lam_init)
            o_s = _diff_sample(q_s, k_s, v_s, cache_k_l2, cache_v_l2, c_lam[j], c_subln_g[j], nbs, ts, lam_init)
            nh = D // LANES
            new_states.append((k_p.reshape(nbp, tp, nh, LANES), v_p.reshape(nbp, tp, nh, LANES),
                               k_s.reshape(nbs, ts, nh, LANES), v_s.reshape(nbs, ts, nh, LANES)))
            w_o = c_w_o[j]

        wr = jnp.concatenate([moe_w_rt[i], moe_w_grp[i], jnp.zeros((D, LANES - N_EXPERTS - N_GROUPS), F32)], axis=1)
        wr_hi = wr.astype(BF16)
        wr_lo = (wr - wr_hi.astype(F32)).astype(BF16)
        wr_hl = jnp.concatenate([wr_hi, wr_lo], axis=1)
        br = jnp.concatenate([moe_b_rt[i], moe_b_grp[i], jnp.zeros((LANES - N_EXPERTS - N_GROUPS,), F32)]).reshape(1, LANES)
        lg0, lb0 = ln_g[i, 0].reshape(1, D), ln_b[i, 0].reshape(1, D)
        lg1, lb1 = ln_g[i, 1].reshape(1, D), ln_b[i, 1].reshape(1, D)
        w_o_b = w_o.astype(BF16)

        xp, hm, rt = _outln(o_p, xp, m_p[2], w_o_b, lg0, lb0, m_p[3], m_p[4], wr_hl, wr_hi, br,
                            False, tm_p, tps_p, n_all)
        xs, hm, rt = _outln(o_s, xs, m_s[2], w_o_b, lg0, lb0, m_s[3], m_s[4], wr_hl, wr_hi, br,
                            True, tm_s, 1, n_all, hm_rt=(hm, rt))

        e_flat = rt[:, :2].astype(I32).reshape(-1)
        block_expert, slot_tok, pos = _dispatch(e_flat)
        yb = _ffn(block_expert, slot_tok, hm, moe_w1[i].astype(BF16), moe_w3[i].astype(BF16), moe_w2[i].astype(BF16))
        xp = _combine(pos[:2 * n_p], yb, rt[:n_p], xp, m_p[5], lg1, lb1, False, tp // min(CMB_TM, n_p))
        xs = _combine(pos[2 * n_p:], yb, rt[n_p:], xs, m_s[5], lg1, lb1, True, 1)

    (k0p, v0p, k0s, v0s), (wkv1p, sh1p, wkv1s, sh1s), (k2p, v2p, k2s, v2s), (k3p, v3p, k3s, v3s) = new_states
    return (xp.reshape(nbp, tp, D), xs.reshape(nbs, ts, D), k0p, v0p, k0s, v0s, wkv1p, sh1p, wkv1s, sh1s,
            k2p, v2p, k2s, v2s, k3p, v3p, k3s, v3s)
```

```python
import functools
import math

import jax
import jax.numpy as jnp
from jax import lax
from jax.experimental import pallas as pl
from jax.experimental.pallas import tpu as pltpu

F32 = jnp.float32
BF16 = jnp.bfloat16
I32 = jnp.int32

D = 1024
DEPTH = 4
LANES = 128
HD = 64

A_HEADS, A_KV = 16, 4
WINDOW, CHUNK = 128, 64
N_EXPERTS, N_GROUPS, EPG, D_EXPERT = 32, 4, 8, 256
GN_EPS = 64e-5
LN_EPS = 1e-5
NEG = -1e30
ALPHA = (2 * DEPTH) ** 0.25

FFN_BLK = 256
CMB_TM = 256
VMEM_LIMIT = 56 * 1024 * 1024


def _cp(sem):
    return pltpu.CompilerParams(dimension_semantics=sem, vmem_limit_bytes=VMEM_LIMIT)


def _dot(a, b):
    return jnp.dot(a, b, preferred_element_type=F32)


def _dot_nt(a, b):
    return lax.dot_general(a, b, (((1,), (1,)), ((), ())), preferred_element_type=F32)


def _split2(x):
    hi = x.astype(BF16)
    lo = (x - hi.astype(F32)).astype(BF16)
    return hi, lo


def _split3(x):
    hi = x.astype(BF16)
    r1 = x - hi.astype(F32)
    mid = r1.astype(BF16)
    lo = (r1 - mid.astype(F32)).astype(BF16)
    return hi, mid, lo


def _dot3(a, b, dot=_dot):
    ah, al = _split2(a)
    bh, bl = _split2(b)
    return dot(ah, bh) + (dot(ah, bl) + dot(al, bh))


def _dot_sel(a, sel_bf16):
    h, m, l = _split3(a)
    return _dot(h, sel_bf16) + (_dot(m, sel_bf16) + _dot(l, sel_bf16))


def _layer_norm(y, g, b):
    mu = jnp.mean(y, -1, keepdims=True)
    yc = y - mu
    var = jnp.mean(yc * yc, -1, keepdims=True)
    return yc * lax.rsqrt(var + LN_EPS) * g + b


def _mod_spec(per_token, tm, tiles_per_seq):
    if per_token:
        return pl.BlockSpec((tm, D), lambda i: (i, 0))
    return pl.BlockSpec((None, 1, D), lambda i: (i // tiles_per_seq, 0, 0))


def _full(shape):
    return pl.BlockSpec(shape, lambda *_: (0,) * len(shape))


def _ada_kernel(c_ref, w_ref, b_ref, o_ref):
    c = c_ref[...]
    s = c * jax.nn.sigmoid(c)
    o_ref[...] = _dot3(s, w_ref[...]) + b_ref[...]


def _ada_all(c_all, w_ada, b_ada):
    nb = c_all.shape[0]
    tn = 1536
    return pl.pallas_call(
        _ada_kernel,
        grid=(DEPTH, 6 * D // tn),
        in_specs=[pl.BlockSpec((nb, D), lambda l, j: (0, 0)),
                  pl.BlockSpec((None, D, tn), lambda l, j: (l, 0, j)),
                  pl.BlockSpec((None, 1, tn), lambda l, j: (l, 0, j))],
        out_specs=pl.BlockSpec((None, nb, tn), lambda l, j: (l, 0, j)),
        out_shape=jax.ShapeDtypeStruct((DEPTH, nb, 6 * D), F32),
        compiler_params=_cp(("arbitrary", "arbitrary")),
        name="ada",
    )(c_all, w_ada, b_ada.reshape(DEPTH, 1, 6 * D))


def _qkv_kernel(x_ref, sh_ref, sc_ref, w_ref, *o_refs, splits):
    h = (x_ref[...] * (1.0 + sc_ref[...]) + sh_ref[...]).astype(BF16)
    for o_ref, (c0, c1) in zip(o_refs, splits):
        o_ref[...] = _dot(h, w_ref[:, c0:c1]).astype(o_ref.dtype)


def _qkv(x, sh, sc, w_bf16, splits, dtypes, per_token, tm, tiles_per_seq):
    n = x.shape[0]
    nout = w_bf16.shape[1]
    mod = _mod_spec(per_token, tm, tiles_per_seq)
    return pl.pallas_call(
        functools.partial(_qkv_kernel, splits=splits),
        grid=(n // tm,),
        in_specs=[pl.BlockSpec((tm, D), lambda i: (i, 0)), mod, mod, _full((D, nout))],
        out_specs=[pl.BlockSpec((tm, c1 - c0), lambda i: (i, 0)) for c0, c1 in splits],
        out_shape=[jax.ShapeDtypeStruct((n, c1 - c0), dt) for (c0, c1), dt in zip(splits, dtypes)],
        compiler_params=_cp(("arbitrary",)),
        name="qkv",
    )(x, sh, sc, w_bf16)


def _gqa_rows(q_slab, k_slabs, v_slabs, sinks_ref, n_invalid):
    outs = [None] * (A_HEADS // 2)
    for j in range(A_KV // 2):
        slabs = [q_slab(4 * j + a) for a in range(4)]
        nq = slabs[0].shape[0]
        half = lax.broadcasted_iota(I32, (nq, LANES), 1) // HD
        aligned = [jnp.where(half == a // 2, q, jnp.zeros_like(q)) for a, q in enumerate(slabs)]
        crossed = [jnp.where(half != a // 2, q, jnp.zeros_like(q)) for a, q in enumerate(slabs)]
        qa = jnp.concatenate(aligned, axis=0) * (HD ** -0.5)
        qc = jnp.concatenate(crossed, axis=0) * (HD ** -0.5)
        s = jnp.concatenate([_dot_nt(qa, k_slabs[j][0]), _dot_nt(qc, k_slabs[j][1])], axis=0)
        if n_invalid is not None:
            col = lax.broadcasted_iota(I32, s.shape, 1)
            s = jnp.where(col >= n_invalid, s, NEG)
        sink = jnp.concatenate(
            [jnp.full((nq, 1), sinks_ref[2 * (4 * j + g % 4) + (g % 4 // 2 if g < 4 else 1 - g % 4 // 2)], F32)
             for g in range(8)], axis=0)
        m = jnp.maximum(jnp.max(s, -1, keepdims=True), sink)
        pe = jnp.exp(s - m)
        den = jnp.sum(pe, -1, keepdims=True) + jnp.exp(sink - m)
        pv = _dot(pe.astype(BF16), v_slabs[j]) / den
        pa = pv[:4 * nq]
        pc = pltpu.roll(pv[4 * nq:], HD, 1)
        for a in range(4):
            ra, rc = pa[a * nq:(a + 1) * nq], pc[a * nq:(a + 1) * nq]
            outs[4 * j + a] = jnp.where(half == a // 2, ra, rc)
    return outs


def _kv_slabs(k_f32, v_f32):
    ks, vs = [], []
    for j in range(A_KV // 2):
        kj = k_f32[:, j * LANES:(j + 1) * LANES]
        ks.append((kj.astype(BF16), pltpu.roll(kj, HD, 1).astype(BF16)))
        vs.append(v_f32[:, j * LANES:(j + 1) * LANES].astype(BF16))
    return ks, vs


def _swa_prompt_kernel(sinks_ref, q_ref, kvp_ref, kvc_ref, o_ref, *, tq):
    i = pl.program_id(1)
    kv = jnp.concatenate([kvp_ref[...], kvc_ref[...]], axis=0)
    kw = A_KV * HD
    ks, vs = _kv_slabs(kv[:, :kw], kv[:, kw:])
    for c in range(tq // CHUNK):
        r0, r1 = c * CHUNK, c * CHUNK + WINDOW + CHUNK
        kc = [(a[r0:r1], b[r0:r1]) for a, b in ks]
        vc = [a[r0:r1] for a in vs]
        n_bad = max(WINDOW - c * CHUNK, 0)
        n_invalid = jnp.where(i == 0, n_bad, 0) if n_bad else None
        outs = _gqa_rows(lambda p: q_ref[c * CHUNK:(c + 1) * CHUNK, p * LANES:(p + 1) * LANES],
                         kc, vc, sinks_ref, n_invalid)
        for p, o in enumerate(outs):
            o_ref[c * CHUNK:(c + 1) * CHUNK, p * LANES:(p + 1) * LANES] = o.astype(o_ref.dtype)


def _swa_prompt(q, kv, sinks, nb, t):
    tq = min(256, t)
    tps = t // tq
    wb = tq // WINDOW
    return pl.pallas_call(
        functools.partial(_swa_prompt_kernel, tq=tq),
        grid=(nb, tps),
        in_specs=[pl.BlockSpec(memory_space=pltpu.SMEM),
                  pl.BlockSpec((tq, D), lambda b, i: (b * tps + i, 0)),
                  pl.BlockSpec((WINDOW, 2 * A_KV * HD), lambda b, i: (jnp.maximum((b * tps + i) * wb - 1, 0), 0)),
                  pl.BlockSpec((tq, 2 * A_KV * HD), lambda b, i: (b * tps + i, 0))],
        out_specs=pl.BlockSpec((tq, D), lambda b, i: (b * tps + i, 0)),
        out_shape=jax.ShapeDtypeStruct((nb * t, D), BF16),
        compiler_params=_cp(("arbitrary", "arbitrary")),
        name="swa_prompt",
    )(sinks, q, kv, kv)


def _swa_sample_kernel(sinks_ref, q_ref, ck_ref, cv_ref, kvn_ref, o_ref):
    kw = A_KV * HD
    kvn = kvn_ref[...]
    k_all = jnp.concatenate([ck_ref[...], kvn[:, :kw]], axis=0)
    v_all = jnp.concatenate([cv_ref[...], kvn[:, kw:]], axis=0)
    ks, vs = _kv_slabs(k_all, v_all)
    outs = _gqa_rows(lambda p: q_ref[:, p * LANES:(p + 1) * LANES], ks, vs, sinks_ref, None)
    for p, o in enumerate(outs):
        o_ref[:, p * LANES:(p + 1) * LANES] = o.astype(o_ref.dtype)


def _swa_sample(q, kv, cache_k, cache_v, sinks, nb, s):
    kw = A_KV * HD
    return pl.pallas_call(
        _swa_sample_kernel,
        grid=(nb,),
        in_specs=[pl.BlockSpec(memory_space=pltpu.SMEM),
                  pl.BlockSpec((s, D), lambda b: (b, 0)),
                  pl.BlockSpec((None, WINDOW, kw), lambda b: (b, 0, 0)),
                  pl.BlockSpec((None, WINDOW, kw), lambda b: (b, 0, 0)),
                  pl.BlockSpec((s, 2 * kw), lambda b: (b, 0))],
        out_specs=pl.BlockSpec((s, D), lambda b: (b, 0)),
        out_shape=jax.ShapeDtypeStruct((nb * s, D), BF16),
        compiler_params=_cp(("arbitrary",)),
        name="swa_sample",
    )(sinks, q, cache_k.reshape(nb, WINDOW, kw), cache_v.reshape(nb, WINDOW, kw), kv)


def _diff_lambda(lam_ref, lam_init):
    l = lam_ref[...]
    return (jnp.exp(jnp.sum(l[0:1] * l[1:2], -1, keepdims=True))
            - jnp.exp(jnp.sum(l[2:3] * l[3:4], -1, keepdims=True)) + lam_init)


def _stack_halves(q):
    half = lax.broadcasted_iota(I32, q.shape, 1) // HD
    z = jnp.zeros_like(q)
    return jnp.concatenate([jnp.where(half == 0, q, z), jnp.where(half == 1, q, z)], axis=0)


def _diff_finish(o2, n, lam_full, sg, lam_init):
    o = o2[:n] - lam_full * o2[n:]
    return o * lax.rsqrt(jnp.mean(o * o, -1, keepdims=True) + LN_EPS) * sg * (1.0 - lam_init)


def _diff_prompt_kernel(lam_ref, q_ref, k_ref, v_ref, sg_ref, o_ref, kb, vb, *, tq, lam_init):
    i = pl.program_id(2)

    @pl.when(i == 0)
    def _():
        kb[...] = k_ref[...].astype(BF16)
        vb[...] = v_ref[...].astype(BF16)

    qs = _stack_halves(q_ref[...])

    def step(j, carry, masked):
        m, l, acc = carry
        off = pl.multiple_of(j * tq, tq)
        s = _dot_nt(qs, kb[pl.ds(off, tq), :]) * (HD ** -0.5)
        if masked:
            row = lax.broadcasted_iota(I32, s.shape, 0) % tq
            col = lax.broadcasted_iota(I32, s.shape, 1)
            s = jnp.where(col // CHUNK <= row // CHUNK, s, NEG)
        m_new = jnp.maximum(m, jnp.max(s, -1, keepdims=True))
        a = jnp.exp(m - m_new)
        p = jnp.exp(s - m_new)
        l = a * l + jnp.sum(p, -1, keepdims=True)
        acc = a * acc + _dot(p.astype(BF16), vb[pl.ds(off, tq), :])
        return m_new, l, acc

    carry = (jnp.full((2 * tq, 1), NEG, F32), jnp.zeros((2 * tq, 1), F32), jnp.zeros((2 * tq, LANES), F32))
    carry = lax.fori_loop(0, i, lambda j, c: step(j, c, False), carry)
    _, l, acc = step(i, carry, True)
    o = _diff_finish(acc / l, tq, _diff_lambda(lam_ref, lam_init), sg_ref[...], lam_init)
    o_ref[...] = o.astype(o_ref.dtype)


def _diff_prompt(q, k, v, lam, subln_g, nb, t, lam_init):
    tq = min(256, t)
    nh = D // LANES
    q3, k3, v3 = (z.reshape(nb, t, D) for z in (q, k, v))
    return pl.pallas_call(
        functools.partial(_diff_prompt_kernel, tq=tq, lam_init=lam_init),
        grid=(nb, nh, t // tq),
        in_specs=[_full((4, HD)),
                  pl.BlockSpec((None, tq, LANES), lambda b, h, i: (b, i, h)),
                  pl.BlockSpec((None, t, LANES), lambda b, h, i: (b, 0, h)),
                  pl.BlockSpec((None, t, LANES), lambda b, h, i: (b, 0, h)),
                  _full((1, LANES))],
        out_specs=pl.BlockSpec((None, tq, LANES), lambda b, h, i: (b, i, h)),
        out_shape=jax.ShapeDtypeStruct((nb, t, D), BF16),
        scratch_shapes=[pltpu.VMEM((t, LANES), BF16), pltpu.VMEM((t, LANES), BF16)],
        compiler_params=_cp(("arbitrary", "arbitrary", "arbitrary")),
        name="diff_prompt",
    )(lam, q3, k3, v3, subln_g.reshape(1, LANES)).reshape(nb * t, D)


def _diff_sample_kernel(lam_ref, q_ref, kn_ref, vn_ref, ck_ref, cv_ref, sg_ref, o_ref, *, lam_init):
    n = q_ref.shape[0]
    qs = _stack_halves(q_ref[...])
    s1 = _dot_nt(qs, ck_ref[...].astype(BF16)) * (HD ** -0.5)
    s2 = _dot_nt(qs, kn_ref[...].astype(BF16)) * (HD ** -0.5)
    m = jnp.maximum(jnp.max(s1, -1, keepdims=True), jnp.max(s2, -1, keepdims=True))
    p1 = jnp.exp(s1 - m)
    p2 = jnp.exp(s2 - m)
    l = jnp.sum(p1, -1, keepdims=True) + jnp.sum(p2, -1, keepdims=True)
    acc = _dot(p1.astype(BF16), cv_ref[...].astype(BF16)) + _dot(p2.astype(BF16), vn_ref[...].astype(BF16))
    o = _diff_finish(acc / l, n, _diff_lambda(lam_ref, lam_init), sg_ref[...], lam_init)
    o_ref[...] = o.astype(o_ref.dtype)


def _diff_sample(q, k, v, cache_k, cache_v, lam, subln_g, nb, s, lam_init):
    nh = D // LANES
    past = cache_k.shape[1]
    q3, k3, v3 = (z.reshape(nb, s, D) for z in (q, k, v))
    new = pl.BlockSpec((None, s, LANES), lambda b, h: (b, 0, h))
    old = pl.BlockSpec((None, past, LANES), lambda b, h: (b, 0, h))
    return pl.pallas_call(
        functools.partial(_diff_sample_kernel, lam_init=lam_init),
        grid=(nb, nh),
        in_specs=[_full((4, HD)), new, new, new, old, old, _full((1, LANES))],
        out_specs=new,
        out_shape=jax.ShapeDtypeStruct((nb, s, D), BF16),
        compiler_params=_cp(("arbitrary", "arbitrary")),
        name="diff_sample",
    )(lam, q3, k3, v3, cache_k.reshape(nb, past, D), cache_v.reshape(nb, past, D),
      subln_g.reshape(1, LANES)).reshape(nb * s, D)


def _rwkv_proj_kernel(x_ref, xp_ref, hp_ref, sh_ref, sc_ref, mu_ref, wrkv_ref, w0_ref, w1_ref, w2_ref,
                      a0_ref, a1_ref, a2_ref, g1_ref, g2_ref,
                      r_ref, k_ref, v_ref, lw_ref, a_ref, g_ref, hl_ref):
    i = pl.program_id(1)
    sc1 = 1.0 + sc_ref[...]
    sh = sh_ref[...]
    h = x_ref[...] * sc1 + sh
    prev_in_seq = xp_ref[7:8, :] * sc1 + sh
    prev = jnp.where(i == 0, hp_ref[...], prev_in_seq)
    row = lax.broadcasted_iota(I32, h.shape, 0)
    hs = jnp.where(row == 0, prev, pltpu.roll(h, 1, 0))
    xx = hs - h
    mu = mu_ref[...]

    def mix(n):
        return (h + xx * mu[n:n + 1]).astype(BF16)

    r_ref[...] = _dot(mix(0), wrkv_ref[0])
    k_ref[...] = _dot(mix(2), wrkv_ref[1])
    v_ref[...] = _dot(mix(3), wrkv_ref[2])
    z = w0_ref[...] + _dot(jnp.tanh(_dot(mix(1), w1_ref[...])).astype(BF16), w2_ref[...])
    softplus_neg = jnp.maximum(-z, 0.0) + jnp.log(1.0 + jnp.exp(-jnp.abs(z)))
    lw_ref[...] = -jnp.exp(-softplus_neg - 0.5)
    a_ref[...] = jax.nn.sigmoid(a0_ref[...] + _dot(_dot(mix(4), a1_ref[...]).astype(BF16), a2_ref[...]))
    g_ref[...] = _dot(jax.nn.sigmoid(_dot(mix(5), g1_ref[...])).astype(BF16), g2_ref[...])
    hl_ref[...] = h[h.shape[0] - 1:, :]


def _rwkv_proj(x, h_prev, sh, sc, wts, nb, t):
    tm = min(512, t)
    tps = t // tm
    x3 = x.reshape(nb, t, D)
    tok = pl.BlockSpec((None, tm, D), lambda b, i: (b, i, 0))
    per_b = pl.BlockSpec((None, 1, D), lambda b, i: (b, 0, 0))
    ins = [tok,
           pl.BlockSpec((None, 8, D), lambda b, i: (b, jnp.maximum(i * (tm // 8) - 1, 0), 0)),
           per_b, per_b, per_b] + [_full(w.shape) for w in wts]
    outs = pl.pallas_call(
        _rwkv_proj_kernel,
        grid=(nb, tps),
        in_specs=ins,
        out_specs=[tok] * 6 + [per_b],
        out_shape=[jax.ShapeDtypeStruct((nb, t, D), F32)] * 6 + [jax.ShapeDtypeStruct((nb, 1, D), F32)],
        compiler_params=_cp(("arbitrary", "arbitrary")),
        name="rwkv_proj",
    )(x3, x3, h_prev.reshape(nb, 1, D), sh, sc, *wts)
    return outs


def _stack(x):
    half = lax.broadcasted_iota(I32, x.shape, 1) // HD
    z = jnp.zeros_like(x)
    return jnp.concatenate([jnp.where(half == 0, x, z), jnp.where(half == 1, x, z)], axis=0)


def _unstack(x2):
    n = x2.shape[0] // 2
    return x2[:n] + x2[n:]


def _rwkv_prep_chunk(r, k, v, lw, a, kk_w, ka_w, rk_w, bd, tri):
    L = r.shape[0]
    n2 = 2 * L

    def seg(x):
        return _dot_sel(x, bd)

    stack = _stack
    kkr = k * kk_w
    kk = kkr / jnp.maximum(jnp.sqrt(seg(kkr * kkr)), 1e-12)
    kp = k * (1.0 + (a - 1.0) * ka_w)
    cum = _dot_sel_lhs(tri, lw)
    g_in = jnp.exp(cum)
    g_ex = jnp.exp(cum - lw)
    g_inv = jnp.exp(-cum)
    g_last = g_in[L - 1:L, :]
    at = kk * g_ex
    bt = kk * a * g_inv
    kt = kp * g_inv
    rt = r * g_in
    a_s, r_s = stack(at), stack(rt)
    ar = jnp.concatenate([a_s, r_s], axis=0)
    bk = jnp.concatenate([stack(bt), stack(kt)], axis=0)
    vs = stack(v)
    pair = _dot3(ar, bk, _dot_nt)
    row = lax.broadcasted_iota(I32, (n2, n2), 0)
    col = lax.broadcasted_iota(I32, (n2, n2), 1)
    strict = col < row
    incl = col <= row
    zero = jnp.zeros((n2, n2), F32)
    x = jnp.where(strict, -pair[:n2, :n2], zero)
    tinv = jnp.where(row == col, 1.0, 0.0).astype(F32) + x
    for _ in range(int(math.log2(L)) - 1):
        x = _dot3(x, x)
        tinv = tinv + _dot3(tinv, x)
    w1 = _dot3(tinv, a_s)
    u0 = _dot3(tinv, _dot3(jnp.where(strict, pair[:n2, n2:], zero), vs))
    mrb = jnp.where(incl, -pair[n2:, :n2], zero)
    w2 = r_s + _dot3(mrb, w1)
    y0 = _dot3(mrb, u0) + _dot3(jnp.where(incl, pair[n2:, n2:], zero), vs)
    bg = stack(bt * g_last)
    rs = lax.broadcasted_iota(I32, (LANES, LANES), 0)
    cs = lax.broadcasted_iota(I32, (LANES, LANES), 1)
    g_mat = jnp.where(rs == cs, g_last, 0.0) - _dot3(w1.T, bg)
    c_mat = _dot3(vs.T, stack(kt * g_last)) - _dot3(u0.T, bg)
    bonus = seg(r * kp * rk_w) * v
    return _unstack(w2), _unstack(y0), bonus, _unstack(g_mat), _unstack(c_mat)


def _dot_sel_lhs(sel_bf16, b):
    h, m, l = _split3(b)
    return _dot(sel_bf16, h) + (_dot(sel_bf16, m) + _dot(sel_bf16, l))


def _block_diag_ones():
    rr = lax.broadcasted_iota(I32, (LANES, LANES), 0)
    cc = lax.broadcasted_iota(I32, (LANES, LANES), 1)
    return jnp.where(rr // HD == cc // HD, 1.0, 0.0).astype(BF16)


def _rwkv_prep_kernel(r_ref, k_ref, v_ref, lw_ref, a_ref, hw_ref, w2_ref, y0_ref, bn_ref, gc_ref, cc_ref,
                      *, L, nchunk, nslab):
    bd = _block_diag_ones()
    rl = lax.broadcasted_iota(I32, (L, L), 0)
    cl = lax.broadcasted_iota(I32, (L, L), 1)
    tri = jnp.where(cl <= rl, 1.0, 0.0).astype(BF16)
    for c in range(nchunk):
        rows = slice(c * L, (c + 1) * L)
        srows = slice(c * HD, (c + 1) * HD)
        for s in range(nslab):
            lanes = slice(s * LANES, (s + 1) * LANES)
            hw = hw_ref[:, lanes]
            w2, y0, bonus, gc, cc = _rwkv_prep_chunk(
                r_ref[rows, lanes], k_ref[rows, lanes], v_ref[rows, lanes], lw_ref[rows, lanes], a_ref[rows, lanes],
                hw[0:1], hw[1:2], hw[2:3], bd, tri)
            w2_ref[rows, lanes] = w2
            y0_ref[rows, lanes] = y0
            bn_ref[rows, lanes] = bonus
            gc_ref[srows, lanes] = gc
            cc_ref[srows, lanes] = cc


def _rwkv_scan_kernel(w2_ref, y0_ref, bn_ref, g_ref, gc_ref, cc_ref, s0_ref, hw_ref, y_ref, sT_ref, s_scr, *, L):
    i = pl.program_id(1)
    tt = w2_ref.shape[0]
    nslab = D // LANES

    @pl.when(i == 0)
    def _():
        s_scr[...] = s0_ref[...]

    bd = _block_diag_ones()

    def chunk(c, carry):
        rows = pl.ds(pl.multiple_of(c * L, L), L)
        srows = pl.ds(pl.multiple_of(c * HD, HD), HD)
        for s in range(nslab):
            lanes = slice(s * LANES, (s + 1) * LANES)
            hw = hw_ref[:, lanes]
            st = s_scr[s]
            y = _unstack(_dot3(_stack(w2_ref[rows, lanes]), st, _dot_nt)) + y0_ref[rows, lanes]
            s_scr[s] = _dot3(st, _stack(gc_ref[srows, lanes])) + _stack(cc_ref[srows, lanes])
            ym = _dot_sel(y, bd) * (1.0 / HD)
            yc = y - ym
            yv = _dot_sel(yc * yc, bd) * (1.0 / HD)
            yn = yc * lax.rsqrt(yv + GN_EPS) * hw[3:4] + hw[4:5]
            y_ref[rows, lanes] = ((yn + bn_ref[rows, lanes]) * g_ref[rows, lanes]).astype(y_ref.dtype)
        return carry

    lax.fori_loop(0, tt // L, chunk, 0)

    @pl.when(i == pl.num_programs(1) - 1)
    def _():
        sT_ref[...] = s_scr[...]


def _rwkv_scan(r, k, v, lw, a, g, s0_bd, head_w, nb, t):
    L = min(64, t)
    nchunk = min(2, t // L)
    nslab = 4
    tt1 = nchunk * L
    w = nslab * LANES
    tok1 = pl.BlockSpec((None, tt1, w), lambda b, sg, i: (b, i, sg))
    cmp1 = pl.BlockSpec((None, nchunk * HD, w), lambda b, sg, i: (b, i, sg))
    ns = (t // L) * HD
    w2, y0, bonus, gc, cc = pl.pallas_call(
        functools.partial(_rwkv_prep_kernel, L=L, nchunk=nchunk, nslab=nslab),
        grid=(nb, D // w, t // tt1),
        in_specs=[tok1] * 5 + [pl.BlockSpec((8, w), lambda b, sg, i: (0, sg))],
        out_specs=[tok1] * 3 + [cmp1] * 2,
        out_shape=[jax.ShapeDtypeStruct((nb, t, D), F32)] * 3 + [jax.ShapeDtypeStruct((nb, ns, D), F32)] * 2,
        compiler_params=_cp(("arbitrary", "arbitrary", "arbitrary")),
        name="rwkv_prep",
    )(r, k, v, lw, a, head_w)

    tt = min(256, t)
    tok = pl.BlockSpec((None, tt, D), lambda b, i: (b, i, 0))
    cmp = pl.BlockSpec((None, (tt // L) * HD, D), lambda b, i: (b, i, 0))
    st = pl.BlockSpec((None, D // LANES, LANES, LANES), lambda b, i: (b, 0, 0, 0))
    y, s_t = pl.pallas_call(
        functools.partial(_rwkv_scan_kernel, L=L),
        grid=(nb, t // tt),
        in_specs=[tok] * 4 + [cmp] * 2 + [st, _full((8, D))],
        out_specs=[tok, st],
        out_shape=[jax.ShapeDtypeStruct((nb, t, D), BF16),
                   jax.ShapeDtypeStruct((nb, D // LANES, LANES, LANES), F32)],
        scratch_shapes=[pltpu.VMEM((D // LANES, LANES, LANES), F32)],
        compiler_params=_cp(("arbitrary", "arbitrary")),
        name="rwkv_scan",
    )(w2, y0, bonus, g, gc, cc, s0_bd, head_w)
    return y.reshape(nb * t, D), s_t


def _state_to_blockdiag(s):
    nb = s.shape[0]
    s = s.astype(F32).reshape(nb, D // LANES, 2, HD, HD)
    z = jnp.zeros_like(s[:, :, 0])
    top = jnp.concatenate([s[:, :, 0], z], axis=-1)
    bot = jnp.concatenate([z, s[:, :, 1]], axis=-1)
    return jnp.concatenate([top, bot], axis=-2)


def _blockdiag_to_state(sb):
    nb = sb.shape[0]
    s0 = sb[:, :, :HD, :HD]
    s1 = sb[:, :, HD:, HD:]
    return jnp.stack([s0, s1], axis=2).reshape(nb, D // HD, HD, HD)


def _route(h, wr_hl_ref, wr_hi_ref, br_ref):
    hh, hl = _split2(h)
    both = _dot(hh, wr_hl_ref[...])
    lg = both[:, :LANES] + both[:, LANES:] + _dot(hl, wr_hi_ref[...]) + br_ref[...]
    lane = lax.broadcasted_iota(I32, lg.shape, 1)
    lane_f = lane.astype(F32)
    big = 1e9
    is_g = (lane >= N_EXPERTS) & (lane < N_EXPERTS + N_GROUPS)
    gl = jnp.where(is_g, lg, -jnp.inf)
    gmax = jnp.max(gl, -1, keepdims=True)
    gidx = jnp.min(jnp.where(gl == gmax, lane_f - N_EXPERTS, big), -1, keepdims=True)
    gw = 1.0 / jnp.sum(jnp.exp(gl - gmax), -1, keepdims=True)
    in_grp = (lane < N_EXPERTS) & ((lane // EPG).astype(F32) == gidx)
    el = jnp.where(in_grp, lg, -jnp.inf)
    m1 = jnp.max(el, -1, keepdims=True)
    i1 = jnp.min(jnp.where(el == m1, lane_f, big), -1, keepdims=True)
    el2 = jnp.where(lane_f == i1, -jnp.inf, el)
    m2 = jnp.max(el2, -1, keepdims=True)
    i2 = jnp.min(jnp.where(el2 == m2, lane_f, big), -1, keepdims=True)
    e2 = jnp.exp(m2 - m1)
    g1 = gw / (1.0 + e2)
    g2 = gw * e2 / (1.0 + e2)
    return jnp.where(lane == 0, i1, jnp.where(lane == 1, i2, jnp.where(lane == 2, g1, jnp.where(lane == 3, g2, 0.0))))


def _outln_kernel(*refs, aliased):
    if aliased:
        refs = refs[2:]
    (o_ref, x_ref, gt_ref, w_ref, g_ref, b_ref, shf_ref, scf_ref, wr_hl_ref, wr_hi_ref, br_ref,
     xn_ref, hm_ref, rt_ref) = refs
    op = _dot(o_ref[...], w_ref[...])
    xn = _layer_norm(ALPHA * x_ref[...] + (1.0 + gt_ref[...]) * op, g_ref[...], b_ref[...])
    xn_ref[...] = xn
    h = xn * (1.0 + scf_ref[...]) + shf_ref[...]
    _store_rows(hm_ref, h)
    rt_ref[...] = _route(h, wr_hl_ref, wr_hi_ref, br_ref)


def _outln(o, x, gt, w_o, ln_g, ln_b, shf, scf, wr_hl, wr_hi, br, per_token, tm, tiles_per_seq, n_all, hm_rt=None):
    n = x.shape[0]
    nt = n // tm
    blk0 = 0 if hm_rt is None else (n_all - n) // tm
    steps = nt if hm_rt is not None else n_all // tm
    if per_token:
        mod = pl.BlockSpec((tm, D), lambda i: (jnp.minimum(i, nt - 1), 0))
    else:
        mod = pl.BlockSpec((None, 1, D), lambda i: (jnp.minimum(i, nt - 1) // tiles_per_seq, 0, 0))
    tok = pl.BlockSpec((tm, D), lambda i: (jnp.minimum(i, nt - 1), 0))
    ins = [tok, tok, mod, _full((D, D)), _full((1, D)), _full((1, D)), mod, mod,
           _full((D, 2 * LANES)), _full((D, LANES)), _full((1, LANES))]
    args = [o, x, gt, w_o, ln_g, ln_b, shf, scf, wr_hl, wr_hi, br]
    aliases = {}
    if hm_rt is not None:
        ins = [pl.BlockSpec(memory_space=pl.ANY)] * 2 + ins
        args = list(hm_rt) + args
        aliases = {0: 1, 1: 2}
    return pl.pallas_call(
        functools.partial(_outln_kernel, aliased=hm_rt is not None),
        grid=(steps,),
        in_specs=ins,
        out_specs=[tok, pl.BlockSpec((tm, D // LANES, LANES), lambda i: (i + blk0, 0, 0)),
                   pl.BlockSpec((tm, LANES), lambda i: (i + blk0, 0))],
        out_shape=[jax.ShapeDtypeStruct((n, D), F32), jax.ShapeDtypeStruct((n_all, D // LANES, LANES), F32),
                   jax.ShapeDtypeStruct((n_all, LANES), F32)],
        input_output_aliases=aliases,
        compiler_params=_cp(("arbitrary",)),
        name="outln",
    )(*args)


def _store_rows(ref3, x):
    for c in range(D // LANES):
        ref3[:, c, :] = x[:, c * LANES:(c + 1) * LANES]


def _load_rows(ref3):
    return jnp.concatenate([ref3[:, c, :] for c in range(D // LANES)], axis=1)


def _gather_rows(idx_ref, n, src_hbm, dst, sem, stride=1, offset=0):
    for r in range(n):
        t = idx_ref[0, r * stride + offset]
        pltpu.make_async_copy(src_hbm.at[t], dst.at[r], sem).start()


def _wait_rows(src_hbm, dst, sem):
    n = dst.shape[0]
    pltpu.make_async_copy(src_hbm.at[pl.ds(0, n)], dst, sem).wait()


def _ffn_kernel(be_ref, tok_ref, tokn_ref, x_hbm, w1_ref, w3_ref, w2_ref, y_ref, buf, sem):
    del be_ref
    i = pl.program_id(0)
    last = pl.num_programs(0) - 1
    slot = i % 2

    @pl.when(i == 0)
    def _():
        _gather_rows(tok_ref, FFN_BLK, x_hbm, buf.at[0], sem.at[0])

    _wait_rows(x_hbm, buf.at[slot], sem.at[slot])
    _gather_rows(tokn_ref, FFN_BLK, x_hbm, buf.at[1 - slot], sem.at[1 - slot])

    xb = _load_rows(buf.at[slot]).astype(BF16)
    a = _dot(xb, w1_ref[...])
    hdn = (a * jax.nn.sigmoid(a)) * _dot(xb, w3_ref[...])
    _store_rows(y_ref, _dot(hdn.astype(BF16), w2_ref[...]))

    @pl.when(i == last)
    def _():
        _wait_rows(x_hbm, buf.at[1 - slot], sem.at[1 - slot])


def _ffn(block_expert, slot_tok, hm, w1, w3, w2):
    nblk = block_expert.shape[0]
    tok3 = slot_tok.reshape(nblk, 1, FFN_BLK)
    grid_spec = pltpu.PrefetchScalarGridSpec(
        num_scalar_prefetch=1,
        grid=(nblk,),
        in_specs=[pl.BlockSpec((None, 1, FFN_BLK), lambda i, be: (i, 0, 0), memory_space=pltpu.SMEM),
                  pl.BlockSpec((None, 1, FFN_BLK), lambda i, be: (jnp.minimum(i + 1, nblk - 1), 0, 0),
                               memory_space=pltpu.SMEM),
                  pl.BlockSpec(memory_space=pl.ANY),
                  pl.BlockSpec((None, D, D_EXPERT), lambda i, be: (be[i], 0, 0)),
                  pl.BlockSpec((None, D, D_EXPERT), lambda i, be: (be[i], 0, 0)),
                  pl.BlockSpec((None, D_EXPERT, D), lambda i, be: (be[i], 0, 0))],
        out_specs=pl.BlockSpec((FFN_BLK, D // LANES, LANES), lambda i, be: (i, 0, 0)),
        scratch_shapes=[pltpu.VMEM((2, FFN_BLK, D // LANES, LANES), F32), pltpu.SemaphoreType.DMA((2,))],
    )
    return pl.pallas_call(
        _ffn_kernel,
        grid_spec=grid_spec,
        out_shape=jax.ShapeDtypeStruct((nblk * FFN_BLK, D // LANES, LANES), F32),
        compiler_params=_cp(("arbitrary",)),
        name="moe_ffn",
    )(block_expert, tok3, tok3, hm, w1, w3, w2)


def _combine_kernel(pos_ref, posn_ref, yb_hbm, rt_ref, x_ref, gt_ref, g_ref, b_ref, o_ref, buf, sem):
    i = pl.program_id(0)
    last = pl.num_programs(0) - 1
    slot = i % 2
    tm = x_ref.shape[0]

    def issue(idx_ref, s):
        _gather_rows(idx_ref, tm, yb_hbm, buf.at[s, 0], sem.at[s], stride=2, offset=0)
        _gather_rows(idx_ref, tm, yb_hbm, buf.at[s, 1], sem.at[s], stride=2, offset=1)

    def drain(s):
        for kk in range(2):
            _wait_rows(yb_hbm, buf.at[s, kk], sem.at[s])

    @pl.when(i == 0)
    def _():
        issue(pos_ref, 0)

    drain(slot)
    issue(posn_ref, 1 - slot)

    rt = rt_ref[...]
    f = rt[:, 2:3] * _load_rows(buf.at[slot, 0]) + rt[:, 3:4] * _load_rows(buf.at[slot, 1])
    o_ref[...] = _layer_norm(ALPHA * x_ref[...] + (1.0 + gt_ref[...]) * f, g_ref[...], b_ref[...])

    @pl.when(i == last)
    def _():
        drain(1 - slot)


def _combine(pos, yb, rt, x, gt, ln_g, ln_b, per_token, tiles_per_seq):
    n = x.shape[0]
    tm = min(CMB_TM, n)
    nt = n // tm
    pos3 = pos.reshape(nt, 1, 2 * tm)
    mod = _mod_spec(per_token, tm, tiles_per_seq)
    tok = pl.BlockSpec((tm, D), lambda i: (i, 0))
    return pl.pallas_call(
        _combine_kernel,
        grid=(nt,),
        in_specs=[pl.BlockSpec((None, 1, 2 * tm), lambda i: (i, 0, 0), memory_space=pltpu.SMEM),
                  pl.BlockSpec((None, 1, 2 * tm), lambda i: (jnp.minimum(i + 1, nt - 1), 0, 0), memory_space=pltpu.SMEM),
                  pl.BlockSpec(memory_space=pl.ANY),
                  pl.BlockSpec((tm, LANES), lambda i: (i, 0)),
                  tok, mod, _full((1, D)), _full((1, D))],
        out_specs=tok,
        out_shape=jax.ShapeDtypeStruct((n, D), F32),
        scratch_shapes=[pltpu.VMEM((2, 2, tm, D // LANES, LANES), F32), pltpu.SemaphoreType.DMA((2,))],
        compiler_params=_cp(("arbitrary",)),
        name="moe_combine",
    )(pos3, pos3, yb, rt, x, gt, ln_g, ln_b)


def _dispatch(e_flat):
    n_assign = e_flat.shape[0]
    ids = jnp.arange(n_assign, dtype=I32)
    experts = jnp.arange(N_EXPERTS, dtype=I32)

    def lookup(table, e):
        return jnp.sum(jnp.where(e[:, None] == experts[None, :], table[None, :], 0), axis=1)

    e_sorted, order = lax.sort((e_flat, ids), num_keys=1, is_stable=True)
    counts = jnp.sum((e_flat[:, None] == experts[None, :]).astype(I32), axis=0)
    start = jnp.cumsum(counts) - counts
    padded = (counts + FFN_BLK - 1) // FFN_BLK * FFN_BLK
    pend = jnp.cumsum(padded)
    pstart = pend - padded
    dest = ids + lookup(pstart - start, e_sorted)
    _, pos = lax.sort((order, dest), num_keys=1)
    n_blocks = -(-(n_assign + N_EXPERTS * (FFN_BLK - 1)) // FFN_BLK)
    blk_start = jnp.arange(n_blocks, dtype=I32) * FFN_BLK
    block_expert = jnp.minimum(jnp.sum((pend[None, :] <= blk_start[:, None]).astype(I32), axis=1), N_EXPERTS - 1)
    e_slot = jnp.repeat(block_expert, FFN_BLK)
    rank = jnp.arange(n_blocks * FFN_BLK, dtype=I32) - lookup(pstart, e_slot)
    valid = rank < lookup(counts, e_slot)
    src = jnp.where(valid, rank + lookup(start, e_slot), 0)
    slot_tok = jnp.where(valid, order[src] // 2, 0)
    return block_expert, slot_tok, pos


def kernel(x_prompt, x_sample, c_prompt, c_sample, cache_k_l0, cache_v_l0, state_wkv_l1, state_shift_l1, cache_k_l2, cache_v_l2, cache_k_l3, cache_v_l3, w_ada, b_ada, ln_g, ln_b, a_w_qkv, a_sinks, a_w_o, b_mu, b_w_rkv, b_w0, b_w1, b_w2, b_a0, b_a1, b_a2, b_g1, b_g2, b_k_k, b_k_a, b_r_k, b_lnx_g, b_lnx_b, b_w_o, c_w_qkv, c_lam, c_subln_g, c_w_o, moe_w_grp, moe_b_grp, moe_w_rt, moe_b_rt, moe_w1, moe_w3, moe_w2):
    nbp, tp, _ = x_prompt.shape
    nbs, ts, _ = x_sample.shape
    n_p, n_s = nbp * tp, nbs * ts
    n_all = n_p + n_s
    tm_p = min(512, tp)
    tps_p = tp // tm_p
    tm_s = n_s
    assert n_p % tm_s == 0 and n_p % CMB_TM == 0

    xp = x_prompt.reshape(n_p, D)
    xs = x_sample.reshape(n_s, D)
    mods = _ada_all(jnp.concatenate([c_prompt, c_sample], axis=0), w_ada, b_ada)

    a_caches = [(cache_k_l0, cache_v_l0), (cache_k_l3, cache_v_l3)]
    new_states = []
    for i in range(DEPTH):
        kind, j = i % 3, i // 3
        m_p = [mods[i, :nbp, n * D:(n + 1) * D].reshape(nbp, 1, D) for n in range(6)]
        m_sb = [mods[i, nbp:, n * D:(n + 1) * D].reshape(nbs, 1, D) for n in range(6)]
        m_s = [jnp.broadcast_to(m, (nbs, ts, D)).reshape(n_s, D) for m in m_sb]

        if kind == 0:
            w = a_w_qkv[j].astype(BF16)
            splits = ((0, D), (D, D + 2 * A_KV * HD))
            q_p, kv_p = _qkv(xp, m_p[0], m_p[1], w, splits, (BF16, F32), False, tm_p, tps_p)
            q_s, kv_s = _qkv(xs, m_s[0], m_s[1], w, splits, (BF16, F32), True, tm_s, 1)
            o_p = _swa_prompt(q_p, kv_p, a_sinks[j], nbp, tp)
            o_s = _swa_sample(q_s, kv_s, a_caches[j][0], a_caches[j][1], a_sinks[j], nbs, ts)
            kw = A_KV * HD
            kv_p3 = kv_p.reshape(nbp, tp, 2 * kw)[:, tp - WINDOW:]
            kv_s3 = kv_s.reshape(nbs, ts, 2 * kw)
            k_s = jnp.concatenate([a_caches[j][0], kv_s3[..., :kw].reshape(nbs, ts, A_KV, HD)], axis=1)[:, ts:]
            v_s = jnp.concatenate([a_caches[j][1], kv_s3[..., kw:].reshape(nbs, ts, A_KV, HD)], axis=1)[:, ts:]
            new_states.append((kv_p3[..., :kw].reshape(nbp, WINDOW, A_KV, HD),
                               kv_p3[..., kw:].reshape(nbp, WINDOW, A_KV, HD), k_s, v_s))
            w_o = a_w_o[j]
        elif kind == 1:
            wts = (b_mu[j], b_w_rkv[j].astype(BF16), b_w0[j].reshape(1, D), b_w1[j].astype(BF16),
                   b_w2[j].astype(BF16), b_a0[j].reshape(1, D), b_a1[j].astype(BF16), b_a2[j].astype(BF16),
                   b_g1[j].astype(BF16), b_g2[j].astype(BF16))
            zero8 = jnp.zeros((3, D), F32)
            head_w = jnp.concatenate([b_k_k[j].reshape(1, D), b_k_a[j].reshape(1, D), b_r_k[j].reshape(1, D),
                                      b_lnx_g[j].reshape(1, D), b_lnx_b[j].reshape(1, D), zero8], axis=0)
            *rk_p, hl_p = _rwkv_proj(xp, jnp.zeros((nbp, D), F32), m_p[0], m_p[1], wts, nbp, tp)
            *rk_s, hl_s = _rwkv_proj(xs, state_shift_l1, m_sb[0], m_sb[1], wts, nbs, ts)
            s0_p = jnp.zeros((nbp, D // LANES, LANES, LANES), F32)
            o_p, st_p = _rwkv_scan(*rk_p, s0_p, head_w, nbp, tp)
            o_s, st_s = _rwkv_scan(*rk_s, _state_to_blockdiag(state_wkv_l1), head_w, nbs, ts)
            new_states.append((_blockdiag_to_state(st_p), hl_p.reshape(nbp, D),
                               _blockdiag_to_state(st_s), hl_s.reshape(nbs, D)))
            w_o = b_w_o[j]
        else:
            lam_init = 0.8 - 0.6 * math.exp(-0.3 * i)
            w = c_w_qkv[j].astype(BF16)
            splits = ((0, D), (D, 2 * D), (2 * D, 3 * D))
            q_p, k_p, v_p = _qkv(xp, m_p[0], m_p[1], w, splits, (BF16, F32, F32), False, tm_p, tps_p)
            q_s, k_s, v_s = _qkv(xs, m_s[0], m_s[1], w, splits, (BF16, F32, F32), True, tm_s, 1)
            o_p = _diff_prompt(q_p, k_p, v_p, c_lam[j], c_subln_g[j], nbp, tp, lam_init)
            o_s = _diff_sample(q_s, k_s, v_s, cache_k_l2, cache_v_l2, c_lam[j], c_subln_g[j], nbs, ts, lam_init)
            nh = D // LANES
            new_states.append((k_p.reshape(nbp, tp, nh, LANES), v_p.reshape(nbp, tp, nh, LANES),
                               k_s.reshape(nbs, ts, nh, LANES), v_s.reshape(nbs, ts, nh, LANES)))
            w_o = c_w_o[j]

        wr = jnp.concatenate([moe_w_rt[i], moe_w_grp[i], jnp.zeros((D, LANES - N_EXPERTS - N_GROUPS), F32)], axis=1)
        wr_hi = wr.astype(BF16)
        wr_lo = (wr - wr_hi.astype(F32)).astype(BF16)
        wr_hl = jnp.concatenate([wr_hi, wr_lo], axis=1)
        br = jnp.concatenate([moe_b_rt[i], moe_b_grp[i], jnp.zeros((LANES - N_EXPERTS - N_GROUPS,), F32)]).reshape(1, LANES)
        lg0, lb0 = ln_g[i, 0].reshape(1, D), ln_b[i, 0].reshape(1, D)
        lg1, lb1 = ln_g[i, 1].reshape(1, D), ln_b[i, 1].reshape(1, D)
        w_o_b = w_o.astype(BF16)

        xp, hm, rt = _outln(o_p, xp, m_p[2], w_o_b, lg0, lb0, m_p[3], m_p[4], wr_hl, wr_hi, br,
                            False, tm_p, tps_p, n_all)
        xs, hm, rt = _outln(o_s, xs, m_s[2], w_o_b, lg0, lb0, m_s[3], m_s[4], wr_hl, wr_hi, br,
                            True, tm_s, 1, n_all, hm_rt=(hm, rt))

        e_flat = rt[:, :2].astype(I32).reshape(-1)
        block_expert, slot_tok, pos = _dispatch(e_flat)
        yb = _ffn(block_expert, slot_tok, hm, moe_w1[i].astype(BF16), moe_w3[i].astype(BF16), moe_w2[i].astype(BF16))
        xp = _combine(pos[:2 * n_p], yb, rt[:n_p], xp, m_p[5], lg1, lb1, False, tp // min(CMB_TM, n_p))
        xs = _combine(pos[2 * n_p:], yb, rt[n_p:], xs, m_s[5], lg1, lb1, True, 1)

    (k0p, v0p, k0s, v0s), (wkv1p, sh1p, wkv1s, sh1s), (k2p, v2p, k2s, v2s), (k3p, v3p, k3s, v3s) = new_states
    return (xp.reshape(nbp, tp, D), xs.reshape(nbs, ts, D), k0p, v0p, k0s, v0s, wkv1p, sh1p, wkv1s, sh1s,
            k2p, v2p, k2s, v2s, k3p, v3p, k3s, v3s)
```

```python
import functools
import math

import jax
import jax.numpy as jnp
from jax import lax
from jax.experimental import pallas as pl
from jax.experimental.pallas import tpu as pltpu

F32 = jnp.float32
BF16 = jnp.bfloat16
I32 = jnp.int32

D = 1024
DEPTH = 4
LANES = 128
HD = 64

A_HEADS, A_KV = 16, 4
WINDOW, CHUNK = 128, 64
N_EXPERTS, N_GROUPS, EPG, D_EXPERT = 32, 4, 8, 256
GN_EPS = 64e-5
LN_EPS = 1e-5
NEG = -1e30
ALPHA = (2 * DEPTH) ** 0.25

FFN_BLK = 256
CMB_TM = 256
VMEM_LIMIT = 56 * 1024 * 1024


def _cp(sem):
    return pltpu.CompilerParams(dimension_semantics=sem, vmem_limit_bytes=VMEM_LIMIT)


def _dot(a, b):
    return jnp.dot(a, b, preferred_element_type=F32)


def _dot_nt(a, b):
    return lax.dot_general(a, b, (((1,), (1,)), ((), ())), preferred_element_type=F32)


def _split2(x):
    hi = x.astype(BF16)
    lo = (x - hi.astype(F32)).astype(BF16)
    return hi, lo


def _split3(x):
    hi = x.astype(BF16)
    r1 = x - hi.astype(F32)
    mid = r1.astype(BF16)
    lo = (r1 - mid.astype(F32)).astype(BF16)
    return hi, mid, lo


def _dot3(a, b, dot=_dot):
    ah, al = _split2(a)
    bh, bl = _split2(b)
    m = a.shape[0]
    if m % 16:
        return dot(ah, bh) + (dot(ah, bl) + dot(al, bh))
    top = dot(jnp.concatenate([ah, al], axis=0), bh)
    return top[:m] + (top[m:] + dot(ah, bl))


def _dot_sel(a, sel_bf16):
    h, m, l = _split3(a)
    return _dot(h, sel_bf16) + (_dot(m, sel_bf16) + _dot(l, sel_bf16))


def _layer_norm(y, g, b):
    mu = jnp.mean(y, -1, keepdims=True)
    yc = y - mu
    var = jnp.mean(yc * yc, -1, keepdims=True)
    return yc * lax.rsqrt(var + LN_EPS) * g + b


def _mod_spec(per_token, tm, tiles_per_seq):
    if per_token:
        return pl.BlockSpec((tm, D), lambda i: (i, 0))
    return pl.BlockSpec((None, 1, D), lambda i: (i // tiles_per_seq, 0, 0))


def _full(shape):
    return pl.BlockSpec(shape, lambda *_: (0,) * len(shape))


def _ada_kernel(c_ref, w_ref, b_ref, o_ref):
    c = c_ref[...]
    s = c * jax.nn.sigmoid(c)
    o_ref[...] = _dot3(s, w_ref[...]) + b_ref[...]


def _ada_all(c_all, w_ada, b_ada):
    nb = c_all.shape[0]
    tn = 1536
    return pl.pallas_call(
        _ada_kernel,
        grid=(DEPTH, 6 * D // tn),
        in_specs=[pl.BlockSpec((nb, D), lambda l, j: (0, 0)),
                  pl.BlockSpec((None, D, tn), lambda l, j: (l, 0, j)),
                  pl.BlockSpec((None, 1, tn), lambda l, j: (l, 0, j))],
        out_specs=pl.BlockSpec((None, nb, tn), lambda l, j: (l, 0, j)),
        out_shape=jax.ShapeDtypeStruct((DEPTH, nb, 6 * D), F32),
        compiler_params=_cp(("arbitrary", "arbitrary")),
        name="ada",
    )(c_all, w_ada, b_ada.reshape(DEPTH, 1, 6 * D))


def _qkv_kernel(x_ref, sh_ref, sc_ref, w_ref, *o_refs, splits):
    h = (x_ref[...] * (1.0 + sc_ref[...]) + sh_ref[...]).astype(BF16)
    for o_ref, (c0, c1) in zip(o_refs, splits):
        o_ref[...] = _dot(h, w_ref[:, c0:c1]).astype(o_ref.dtype)


def _qkv(x, sh, sc, w_bf16, splits, dtypes, per_token, tm, tiles_per_seq):
    n = x.shape[0]
    nout = w_bf16.shape[1]
    mod = _mod_spec(per_token, tm, tiles_per_seq)
    return pl.pallas_call(
        functools.partial(_qkv_kernel, splits=splits),
        grid=(n // tm,),
        in_specs=[pl.BlockSpec((tm, D), lambda i: (i, 0)), mod, mod, _full((D, nout))],
        out_specs=[pl.BlockSpec((tm, c1 - c0), lambda i: (i, 0)) for c0, c1 in splits],
        out_shape=[jax.ShapeDtypeStruct((n, c1 - c0), dt) for (c0, c1), dt in zip(splits, dtypes)],
        compiler_params=_cp(("arbitrary",)),
        name="qkv",
    )(x, sh, sc, w_bf16)


def _gqa_rows(q_slab, k_slabs, v_slabs, sinks_ref, n_invalid):
    def pair(j):
        outs = [None] * 4
        slabs = [q_slab(4 * j + a) for a in range(4)]
        nq = slabs[0].shape[0]
        half = lax.broadcasted_iota(I32, (nq, LANES), 1) // HD
        aligned = [jnp.where(half == a // 2, q, jnp.zeros_like(q)) for a, q in enumerate(slabs)]
        crossed = [jnp.where(half != a // 2, q, jnp.zeros_like(q)) for a, q in enumerate(slabs)]
        qa = jnp.concatenate(aligned, axis=0) * (HD ** -0.5)
        qc = jnp.concatenate(crossed, axis=0) * (HD ** -0.5)
        s = jnp.concatenate([_dot_nt(qa, k_slabs[j][0]), _dot_nt(qc, k_slabs[j][1])], axis=0)
        yield
        if n_invalid is not None:
            col = lax.broadcasted_iota(I32, s.shape, 1)
            s = jnp.where(col >= n_invalid, s, NEG)
        sink = jnp.concatenate(
            [jnp.full((nq, 1), sinks_ref[2 * (4 * j + g % 4) + (g % 4 // 2 if g < 4 else 1 - g % 4 // 2)], F32)
             for g in range(8)], axis=0)
        m = jnp.maximum(jnp.max(s, -1, keepdims=True), sink)
        yield
        pe = jnp.exp(s - m)
        den = jnp.sum(pe, -1, keepdims=True) + jnp.exp(sink - m)
        pv = _dot(pe.astype(BF16), v_slabs[j])
        yield
        pv = pv / den
        pa = pv[:4 * nq]
        pc = pltpu.roll(pv[4 * nq:], HD, 1)
        for a in range(4):
            ra, rc = pa[a * nq:(a + 1) * nq], pc[a * nq:(a + 1) * nq]
            outs[a] = jnp.where(half == a // 2, ra, rc)
        return outs

    return [pair(j) for j in range(A_KV // 2)]


def _kv_slabs(k_f32, v_f32):
    ks, vs = [], []
    for j in range(A_KV // 2):
        kj = k_f32[:, j * LANES:(j + 1) * LANES]
        ks.append((kj.astype(BF16), pltpu.roll(kj, HD, 1).astype(BF16)))
        vs.append(v_f32[:, j * LANES:(j + 1) * LANES].astype(BF16))
    return ks, vs


def _swa_prompt_kernel(sinks_ref, q_ref, kvp_ref, kvc_ref, o_ref, *, tq):
    i = pl.program_id(1)
    kv = jnp.concatenate([kvp_ref[...], kvc_ref[...]], axis=0)
    kw = A_KV * HD
    ks, vs = _kv_slabs(kv[:, :kw], kv[:, kw:])
    gens = []
    for c in range(tq // CHUNK):
        r0, r1 = c * CHUNK, c * CHUNK + WINDOW + CHUNK
        kc = [(a[r0:r1], b[r0:r1]) for a, b in ks]
        vc = [a[r0:r1] for a in vs]
        n_bad = max(WINDOW - c * CHUNK, 0)
        n_invalid = jnp.where(i == 0, n_bad, 0) if n_bad else None
        gens += _gqa_rows(lambda p, c=c: q_ref[c * CHUNK:(c + 1) * CHUNK, p * LANES:(p + 1) * LANES],
                          kc, vc, sinks_ref, n_invalid)
    for n, outs in enumerate(_lockstep(gens)):
        c, j = n // 2, n % 2
        for a, o in enumerate(outs):
            p = 4 * j + a
            o_ref[c * CHUNK:(c + 1) * CHUNK, p * LANES:(p + 1) * LANES] = o.astype(o_ref.dtype)


def _swa_prompt(q, kv, sinks, nb, t):
    tq = min(256, t)
    tps = t // tq
    wb = tq // WINDOW
    return pl.pallas_call(
        functools.partial(_swa_prompt_kernel, tq=tq),
        grid=(nb, tps),
        in_specs=[pl.BlockSpec(memory_space=pltpu.SMEM),
                  pl.BlockSpec((tq, D), lambda b, i: (b * tps + i, 0)),
                  pl.BlockSpec((WINDOW, 2 * A_KV * HD), lambda b, i: (jnp.maximum((b * tps + i) * wb - 1, 0), 0)),
                  pl.BlockSpec((tq, 2 * A_KV * HD), lambda b, i: (b * tps + i, 0))],
        out_specs=pl.BlockSpec((tq, D), lambda b, i: (b * tps + i, 0)),
        out_shape=jax.ShapeDtypeStruct((nb * t, D), BF16),
        compiler_params=_cp(("arbitrary", "arbitrary")),
        name="swa_prompt",
    )(sinks, q, kv, kv)


def _swa_sample_kernel(sinks_ref, q_ref, ck_ref, cv_ref, kvn_ref, o_ref):
    kw = A_KV * HD
    kvn = kvn_ref[...]
    k_all = jnp.concatenate([ck_ref[...], kvn[:, :kw]], axis=0)
    v_all = jnp.concatenate([cv_ref[...], kvn[:, kw:]], axis=0)
    ks, vs = _kv_slabs(k_all, v_all)
    pairs = _lockstep(_gqa_rows(lambda p: q_ref[:, p * LANES:(p + 1) * LANES], ks, vs, sinks_ref, None))
    for p, o in enumerate(pairs[0] + pairs[1]):
        o_ref[:, p * LANES:(p + 1) * LANES] = o.astype(o_ref.dtype)


def _swa_sample(q, kv, cache_k, cache_v, sinks, nb, s):
    kw = A_KV * HD
    return pl.pallas_call(
        _swa_sample_kernel,
        grid=(nb,),
        in_specs=[pl.BlockSpec(memory_space=pltpu.SMEM),
                  pl.BlockSpec((s, D), lambda b: (b, 0)),
                  pl.BlockSpec((None, WINDOW, kw), lambda b: (b, 0, 0)),
                  pl.BlockSpec((None, WINDOW, kw), lambda b: (b, 0, 0)),
                  pl.BlockSpec((s, 2 * kw), lambda b: (b, 0))],
        out_specs=pl.BlockSpec((s, D), lambda b: (b, 0)),
        out_shape=jax.ShapeDtypeStruct((nb * s, D), BF16),
        compiler_params=_cp(("arbitrary",)),
        name="swa_sample",
    )(sinks, q, cache_k.reshape(nb, WINDOW, kw), cache_v.reshape(nb, WINDOW, kw), kv)


def _diff_lambda(lam_ref, lam_init):
    l = lam_ref[...]
    return (jnp.exp(jnp.sum(l[0:1] * l[1:2], -1, keepdims=True))
            - jnp.exp(jnp.sum(l[2:3] * l[3:4], -1, keepdims=True)) + lam_init)


def _stack_halves(q):
    half = lax.broadcasted_iota(I32, q.shape, 1) // HD
    z = jnp.zeros_like(q)
    return jnp.concatenate([jnp.where(half == 0, q, z), jnp.where(half == 1, q, z)], axis=0)


def _diff_finish(o2, n, lam_full, sg, lam_init):
    o = o2[:n] - lam_full * o2[n:]
    return o * lax.rsqrt(jnp.mean(o * o, -1, keepdims=True) + LN_EPS) * sg * (1.0 - lam_init)


def _diff_prompt_kernel(lam_ref, q_ref, k_ref, v_ref, sg_ref, o_ref, kb, vtb, *, tq, lam_init):
    i = pl.program_id(2)
    nhs = kb.shape[0]

    @pl.when(i == 0)
    def _():
        for hh in range(nhs):
            lanes = slice(hh * LANES, (hh + 1) * LANES)
            for c in range(kb.shape[1]):
                kb[hh, c] = k_ref[c * tq:(c + 1) * tq, lanes].astype(BF16)
                vtb[hh, c] = v_ref[c * tq:(c + 1) * tq, lanes].T.astype(BF16)

    qs = [_stack_halves(q_ref[:, hh * LANES:(hh + 1) * LANES]) * (HD ** -0.5) for hh in range(nhs)]

    def head_step(hh, j, carry, masked):
        m, l, acc = carry
        st = _dot_nt(kb[hh, j], qs[hh])
        yield
        if masked:
            kc = lax.broadcasted_iota(I32, st.shape, 0) // CHUNK
            qc = (lax.broadcasted_iota(I32, st.shape, 1) % tq) // CHUNK
            st = jnp.where(kc <= qc, st, NEG)
        m_new = jnp.maximum(m, jnp.max(st, 0, keepdims=True))
        yield
        p = jnp.exp(st - m_new)
        pv = _dot(vtb[hh, j], p.astype(BF16))
        yield
        a = jnp.exp(m - m_new)
        return m_new, a * l + jnp.sum(p, 0, keepdims=True), a * acc + pv

    def step(j, carries, masked):
        return tuple(_lockstep([head_step(hh, j, carries[hh], masked) for hh in range(nhs)]))

    init = (jnp.full((1, 2 * tq), NEG, F32), jnp.zeros((1, 2 * tq), F32), jnp.zeros((LANES, 2 * tq), F32))
    carries = lax.fori_loop(0, i, lambda j, c: step(j, c, False), (init,) * nhs)
    lam = _diff_lambda(lam_ref, lam_init)
    for hh, (_, l, acc) in enumerate(step(i, carries, True)):
        o2t = acc / l
        o = (o2t[:, :tq] - lam * o2t[:, tq:]).T
        o = o * lax.rsqrt(jnp.mean(o * o, -1, keepdims=True) + LN_EPS) * sg_ref[...] * (1.0 - lam_init)
        o_ref[:, hh * LANES:(hh + 1) * LANES] = o.astype(o_ref.dtype)


def _diff_prompt(q, k, v, lam, subln_g, nb, t, lam_init):
    tq = min(256, t)
    nhs = 4
    w = nhs * LANES
    q3, k3, v3 = (z.reshape(nb, t, D) for z in (q, k, v))
    return pl.pallas_call(
        functools.partial(_diff_prompt_kernel, tq=tq, lam_init=lam_init),
        grid=(nb, D // w, t // tq),
        in_specs=[_full((4, HD)),
                  pl.BlockSpec((None, tq, w), lambda b, h, i: (b, i, h)),
                  pl.BlockSpec((None, t, w), lambda b, h, i: (b, 0, h)),
                  pl.BlockSpec((None, t, w), lambda b, h, i: (b, 0, h)),
                  _full((1, LANES))],
        out_specs=pl.BlockSpec((None, tq, w), lambda b, h, i: (b, i, h)),
        out_shape=jax.ShapeDtypeStruct((nb, t, D), BF16),
        scratch_shapes=[pltpu.VMEM((nhs, t // tq, tq, LANES), BF16), pltpu.VMEM((nhs, t // tq, LANES, tq), BF16)],
        compiler_params=_cp(("arbitrary", "arbitrary", "arbitrary")),
        name="diff_prompt",
    )(lam, q3, k3, v3, subln_g.reshape(1, LANES)).reshape(nb * t, D)


def _diff_sample_kernel(lam_ref, q_ref, kn_ref, vn_ref, ck_ref, cv_ref, sg_ref, o_ref, *, lam_init):
    n = q_ref.shape[0]
    qs = _stack_halves(q_ref[...])
    s1 = _dot_nt(qs, ck_ref[...].astype(BF16)) * (HD ** -0.5)
    s2 = _dot_nt(qs, kn_ref[...].astype(BF16)) * (HD ** -0.5)
    m = jnp.maximum(jnp.max(s1, -1, keepdims=True), jnp.max(s2, -1, keepdims=True))
    p1 = jnp.exp(s1 - m)
    p2 = jnp.exp(s2 - m)
    l = jnp.sum(p1, -1, keepdims=True) + jnp.sum(p2, -1, keepdims=True)
    acc = _dot(p1.astype(BF16), cv_ref[...].astype(BF16)) + _dot(p2.astype(BF16), vn_ref[...].astype(BF16))
    o = _diff_finish(acc / l, n, _diff_lambda(lam_ref, lam_init), sg_ref[...], lam_init)
    o_ref[...] = o.astype(o_ref.dtype)


def _diff_sample(q, k, v, cache_k, cache_v, lam, subln_g, nb, s, lam_init):
    nh = D // LANES
    past = cache_k.shape[1]
    q3, k3, v3 = (z.reshape(nb, s, D) for z in (q, k, v))
    new = pl.BlockSpec((None, s, LANES), lambda b, h: (b, 0, h))
    old = pl.BlockSpec((None, past, LANES), lambda b, h: (b, 0, h))
    return pl.pallas_call(
        functools.partial(_diff_sample_kernel, lam_init=lam_init),
        grid=(nb, nh),
        in_specs=[_full((4, HD)), new, new, new, old, old, _full((1, LANES))],
        out_specs=new,
        out_shape=jax.ShapeDtypeStruct((nb, s, D), BF16),
        compiler_params=_cp(("arbitrary", "arbitrary")),
        name="diff_sample",
    )(lam, q3, k3, v3, cache_k.reshape(nb, past, D), cache_v.reshape(nb, past, D),
      subln_g.reshape(1, LANES)).reshape(nb * s, D)


def _rwkv_proj_kernel(x_ref, xp_ref, hp_ref, sh_ref, sc_ref, mu_ref, wrkv_ref, w0_ref, w1_ref, w2_ref,
                      a0_ref, a1_ref, a2_ref, g1_ref, g2_ref,
                      r_ref, k_ref, v_ref, lw_ref, a_ref, g_ref, hl_ref):
    i = pl.program_id(1)
    sc1 = 1.0 + sc_ref[...]
    sh = sh_ref[...]
    h = x_ref[...] * sc1 + sh
    prev_in_seq = xp_ref[7:8, :] * sc1 + sh
    prev = jnp.where(i == 0, hp_ref[...], prev_in_seq)
    row = lax.broadcasted_iota(I32, h.shape, 0)
    hs = jnp.where(row == 0, prev, pltpu.roll(h, 1, 0))
    xx = hs - h
    mu = mu_ref[...]

    def mix(n):
        return (h + xx * mu[n:n + 1]).astype(BF16)

    r_ref[...] = _dot(mix(0), wrkv_ref[0])
    k_ref[...] = _dot(mix(2), wrkv_ref[1])
    v_ref[...] = _dot(mix(3), wrkv_ref[2])
    z = w0_ref[...] + _dot(jnp.tanh(_dot(mix(1), w1_ref[...])).astype(BF16), w2_ref[...])
    softplus_neg = jnp.maximum(-z, 0.0) + jnp.log(1.0 + jnp.exp(-jnp.abs(z)))
    lw_ref[...] = -jnp.exp(-softplus_neg - 0.5)
    a_ref[...] = jax.nn.sigmoid(a0_ref[...] + _dot(_dot(mix(4), a1_ref[...]).astype(BF16), a2_ref[...]))
    g_ref[...] = _dot(jax.nn.sigmoid(_dot(mix(5), g1_ref[...])).astype(BF16), g2_ref[...])
    hl_ref[...] = h[h.shape[0] - 1:, :]


def _rwkv_proj(x, h_prev, sh, sc, wts, nb, t):
    tm = min(512, t)
    tps = t // tm
    x3 = x.reshape(nb, t, D)
    tok = pl.BlockSpec((None, tm, D), lambda b, i: (b, i, 0))
    per_b = pl.BlockSpec((None, 1, D), lambda b, i: (b, 0, 0))
    ins = [tok,
           pl.BlockSpec((None, 8, D), lambda b, i: (b, jnp.maximum(i * (tm // 8) - 1, 0), 0)),
           per_b, per_b, per_b] + [_full(w.shape) for w in wts]
    outs = pl.pallas_call(
        _rwkv_proj_kernel,
        grid=(nb, tps),
        in_specs=ins,
        out_specs=[tok] * 6 + [per_b],
        out_shape=[jax.ShapeDtypeStruct((nb, t, D), F32)] * 6 + [jax.ShapeDtypeStruct((nb, 1, D), F32)],
        compiler_params=_cp(("arbitrary", "arbitrary")),
        name="rwkv_proj",
    )(x3, x3, h_prev.reshape(nb, 1, D), sh, sc, *wts)
    return outs


def _stack(x):
    half = lax.broadcasted_iota(I32, x.shape, 1) // HD
    z = jnp.zeros_like(x)
    return jnp.concatenate([jnp.where(half == 0, x, z), jnp.where(half == 1, x, z)], axis=0)


def _unstack(x2):
    n = x2.shape[0] // 2
    return x2[:n] + x2[n:]


def _rwkv_prep_chunk(r, k, v, lw, a, kk_w, ka_w, rk_w, bd, tri):
    L = r.shape[0]
    n2 = 2 * L

    def seg(x):
        return _dot_sel(x, bd)

    stack = _stack
    kkr = k * kk_w
    kk = kkr / jnp.maximum(jnp.sqrt(seg(kkr * kkr)), 1e-12)
    kp = k * (1.0 + (a - 1.0) * ka_w)
    cum = _dot_sel_lhs(tri, lw)
    yield
    g_in = jnp.exp(cum)
    g_ex = jnp.exp(cum - lw)
    g_inv = jnp.exp(-cum)
    g_last = g_in[L - 1:L, :]
    at = kk * g_ex
    bt = kk * a * g_inv
    kt = kp * g_inv
    rt = r * g_in
    a_s, r_s = stack(at), stack(rt)
    ar = jnp.concatenate([a_s, r_s], axis=0)
    bk = jnp.concatenate([stack(bt), stack(kt)], axis=0)
    vs = stack(v)
    pair = _dot3(ar, bk, _dot_nt)
    yield
    row = lax.broadcasted_iota(I32, (n2, n2), 0)
    col = lax.broadcasted_iota(I32, (n2, n2), 1)
    strict = col < row
    incl = col <= row
    zero = jnp.zeros((n2, n2), F32)
    x = jnp.where(strict, -pair[:n2, :n2], zero)
    tinv = jnp.where(row == col, 1.0, 0.0).astype(F32) + x
    x = _dot3(x, x)
    kv = _dot3(jnp.concatenate([jnp.where(strict, pair[:n2, n2:], zero),
                                jnp.where(incl, pair[n2:, n2:], zero)], axis=0), vs)
    yield
    for _ in range(int(math.log2(L)) - 2):
        both = _dot3(jnp.concatenate([tinv, x], axis=0), x)
        tinv = tinv + both[:n2]
        x = both[n2:]
        yield
    tinv = tinv + _dot3(tinv, x)
    c_vk = _dot3(vs.T, stack(kt * g_last))
    bonus = seg(r * kp * rk_w) * v
    yield
    wu = _dot3(tinv, jnp.concatenate([a_s, kv[:n2]], axis=1))
    yield
    mw = _dot3(jnp.where(incl, -pair[n2:, :n2], zero), wu)
    gb = _dot3(wu.T, stack(bt * g_last))
    yield
    w2 = r_s + mw[:, :LANES]
    y0 = mw[:, LANES:] + kv[n2:]
    rs = lax.broadcasted_iota(I32, (LANES, LANES), 0)
    cs = lax.broadcasted_iota(I32, (LANES, LANES), 1)
    g_mat = jnp.where(rs == cs, g_last, 0.0) - gb[:LANES]
    c_mat = c_vk - gb[LANES:]
    return _unstack(w2), _unstack(y0), bonus, _unstack(g_mat), _unstack(c_mat)


def _lockstep(gens):
    results = [None] * len(gens)
    live = list(range(len(gens)))
    while live:
        still = []
        for n in live:
            try:
                next(gens[n])
                still.append(n)
            except StopIteration as stop:
                results[n] = stop.value
        live = still
    return results


def _dot_sel_lhs(sel_bf16, b):
    h, m, l = _split3(b)
    return _dot(sel_bf16, h) + (_dot(sel_bf16, m) + _dot(sel_bf16, l))


def _block_diag_ones():
    rr = lax.broadcasted_iota(I32, (LANES, LANES), 0)
    cc = lax.broadcasted_iota(I32, (LANES, LANES), 1)
    return jnp.where(rr // HD == cc // HD, 1.0, 0.0).astype(BF16)


def _rwkv_prep_kernel(r_ref, k_ref, v_ref, lw_ref, a_ref, hw_ref, w2_ref, y0_ref, bn_ref, gc_ref, cc_ref,
                      *, L, nchunk, nslab):
    bd = _block_diag_ones()
    rl = lax.broadcasted_iota(I32, (L, L), 0)
    cl = lax.broadcasted_iota(I32, (L, L), 1)
    tri = jnp.where(cl <= rl, 1.0, 0.0).astype(BF16)
    where, gens = [], []
    for c in range(nchunk):
        rows = slice(c * L, (c + 1) * L)
        for s in range(nslab):
            lanes = slice(s * LANES, (s + 1) * LANES)
            hw = hw_ref[:, lanes]
            where.append((rows, slice(c * HD, (c + 1) * HD), lanes))
            gens.append(_rwkv_prep_chunk(
                r_ref[rows, lanes], k_ref[rows, lanes], v_ref[rows, lanes], lw_ref[rows, lanes], a_ref[rows, lanes],
                hw[0:1], hw[1:2], hw[2:3], bd, tri))
    for (rows, srows, lanes), (w2, y0, bonus, gc, cc) in zip(where, _lockstep(gens)):
        w2_ref[rows, lanes] = w2
        y0_ref[rows, lanes] = y0
        bn_ref[rows, lanes] = bonus
        gc_ref[srows, lanes] = gc
        cc_ref[srows, lanes] = cc


def _rwkv_scan_kernel(w2_ref, y0_ref, bn_ref, g_ref, gc_ref, cc_ref, s0_ref, hw_ref, y_ref, sT_ref, s_scr, *, L):
    i = pl.program_id(1)
    tt = w2_ref.shape[0]
    nslab = D // LANES

    @pl.when(i == 0)
    def _():
        s_scr[...] = s0_ref[...]

    bd = _block_diag_ones()

    def chunk(c, carry):
        rows = pl.ds(pl.multiple_of(c * L, L), L)
        srows = pl.ds(pl.multiple_of(c * HD, HD), HD)

        def slab(s):
            lanes = slice(s * LANES, (s + 1) * LANES)
            hw = hw_ref[:, lanes]
            st = s_scr[s]
            ys = _dot3(_stack(w2_ref[rows, lanes]), st, _dot_nt)
            s_scr[s] = _dot3(st, _stack(gc_ref[srows, lanes])) + _stack(cc_ref[srows, lanes])
            yield
            y = _unstack(ys) + y0_ref[rows, lanes]
            ym = _dot_sel(y, bd) * (1.0 / HD)
            yield
            yc = y - ym
            yv = _dot_sel(yc * yc, bd) * (1.0 / HD)
            yield
            yn = yc * lax.rsqrt(yv + GN_EPS) * hw[3:4] + hw[4:5]
            y_ref[rows, lanes] = ((yn + bn_ref[rows, lanes]) * g_ref[rows, lanes]).astype(y_ref.dtype)

        _lockstep([slab(s) for s in range(nslab)])
        return carry

    lax.fori_loop(0, tt // L, chunk, 0)

    @pl.when(i == pl.num_programs(1) - 1)
    def _():
        sT_ref[...] = s_scr[...]


def _rwkv_scan(r, k, v, lw, a, g, s0_bd, head_w, nb, t):
    L = min(64, t)
    nchunk = min(2, t // L)
    nslab = 4
    tt1 = nchunk * L
    w = nslab * LANES
    tok1 = pl.BlockSpec((None, tt1, w), lambda b, sg, i: (b, i, sg))
    cmp1 = pl.BlockSpec((None, nchunk * HD, w), lambda b, sg, i: (b, i, sg))
    ns = (t // L) * HD
    w2, y0, bonus, gc, cc = pl.pallas_call(
        functools.partial(_rwkv_prep_kernel, L=L, nchunk=nchunk, nslab=nslab),
        grid=(nb, D // w, t // tt1),
        in_specs=[tok1] * 5 + [pl.BlockSpec((8, w), lambda b, sg, i: (0, sg))],
        out_specs=[tok1] * 3 + [cmp1] * 2,
        out_shape=[jax.ShapeDtypeStruct((nb, t, D), F32)] * 3 + [jax.ShapeDtypeStruct((nb, ns, D), F32)] * 2,
        compiler_params=_cp(("arbitrary", "arbitrary", "arbitrary")),
        name="rwkv_prep",
    )(r, k, v, lw, a, head_w)

    tt = min(256, t)
    tok = pl.BlockSpec((None, tt, D), lambda b, i: (b, i, 0))
    cmp = pl.BlockSpec((None, (tt // L) * HD, D), lambda b, i: (b, i, 0))
    st = pl.BlockSpec((None, D // LANES, LANES, LANES), lambda b, i: (b, 0, 0, 0))
    y, s_t = pl.pallas_call(
        functools.partial(_rwkv_scan_kernel, L=L),
        grid=(nb, t // tt),
        in_specs=[tok] * 4 + [cmp] * 2 + [st, _full((8, D))],
        out_specs=[tok, st],
        out_shape=[jax.ShapeDtypeStruct((nb, t, D), BF16),
                   jax.ShapeDtypeStruct((nb, D // LANES, LANES, LANES), F32)],
        scratch_shapes=[pltpu.VMEM((D // LANES, LANES, LANES), F32)],
        compiler_params=_cp(("arbitrary", "arbitrary")),
        name="rwkv_scan",
    )(w2, y0, bonus, g, gc, cc, s0_bd, head_w)
    return y.reshape(nb * t, D), s_t


def _state_to_blockdiag(s):
    nb = s.shape[0]
    s = s.astype(F32).reshape(nb, D // LANES, 2, HD, HD)
    z = jnp.zeros_like(s[:, :, 0])
    top = jnp.concatenate([s[:, :, 0], z], axis=-1)
    bot = jnp.concatenate([z, s[:, :, 1]], axis=-1)
    return jnp.concatenate([top, bot], axis=-2)


def _blockdiag_to_state(sb):
    nb = sb.shape[0]
    s0 = sb[:, :, :HD, :HD]
    s1 = sb[:, :, HD:, HD:]
    return jnp.stack([s0, s1], axis=2).reshape(nb, D // HD, HD, HD)


def _route(h, wr_hl_ref, wr_hi_ref, br_ref):
    hh, hl = _split2(h)
    both = _dot(hh, wr_hl_ref[...])
    lg = both[:, :LANES] + both[:, LANES:] + _dot(hl, wr_hi_ref[...]) + br_ref[...]
    lane = lax.broadcasted_iota(I32, lg.shape, 1)
    lane_f = lane.astype(F32)
    big = 1e9
    is_g = (lane >= N_EXPERTS) & (lane < N_EXPERTS + N_GROUPS)
    gl = jnp.where(is_g, lg, -jnp.inf)
    gmax = jnp.max(gl, -1, keepdims=True)
    gidx = jnp.min(jnp.where(gl == gmax, lane_f - N_EXPERTS, big), -1, keepdims=True)
    gw = 1.0 / jnp.sum(jnp.exp(gl - gmax), -1, keepdims=True)
    in_grp = (lane < N_EXPERTS) & ((lane // EPG).astype(F32) == gidx)
    el = jnp.where(in_grp, lg, -jnp.inf)
    m1 = jnp.max(el, -1, keepdims=True)
    i1 = jnp.min(jnp.where(el == m1, lane_f, big), -1, keepdims=True)
    el2 = jnp.where(lane_f == i1, -jnp.inf, el)
    m2 = jnp.max(el2, -1, keepdims=True)
    i2 = jnp.min(jnp.where(el2 == m2, lane_f, big), -1, keepdims=True)
    e2 = jnp.exp(m2 - m1)
    g1 = gw / (1.0 + e2)
    g2 = gw * e2 / (1.0 + e2)
    return jnp.where(lane == 0, i1, jnp.where(lane == 1, i2, jnp.where(lane == 2, g1, jnp.where(lane == 3, g2, 0.0))))


def _outln_kernel(*refs, aliased):
    if aliased:
        refs = refs[2:]
    (o_ref, x_ref, gt_ref, w_ref, g_ref, b_ref, shf_ref, scf_ref, wr_hl_ref, wr_hi_ref, br_ref,
     xn_ref, hm_ref, rt_ref) = refs
    op = _dot(o_ref[...], w_ref[...])
    xn = _layer_norm(ALPHA * x_ref[...] + (1.0 + gt_ref[...]) * op, g_ref[...], b_ref[...])
    xn_ref[...] = xn
    h = xn * (1.0 + scf_ref[...]) + shf_ref[...]
    _store_rows(hm_ref, h)
    rt_ref[...] = _route(h, wr_hl_ref, wr_hi_ref, br_ref)


def _outln(o, x, gt, w_o, ln_g, ln_b, shf, scf, wr_hl, wr_hi, br, per_token, tm, tiles_per_seq, n_all, hm_rt=None):
    n = x.shape[0]
    nt = n // tm
    blk0 = 0 if hm_rt is None else (n_all - n) // tm
    steps = nt if hm_rt is not None else n_all // tm
    if per_token:
        mod = pl.BlockSpec((tm, D), lambda i: (jnp.minimum(i, nt - 1), 0))
    else:
        mod = pl.BlockSpec((None, 1, D), lambda i: (jnp.minimum(i, nt - 1) // tiles_per_seq, 0, 0))
    tok = pl.BlockSpec((tm, D), lambda i: (jnp.minimum(i, nt - 1), 0))
    ins = [tok, tok, mod, _full((D, D)), _full((1, D)), _full((1, D)), mod, mod,
           _full((D, 2 * LANES)), _full((D, LANES)), _full((1, LANES))]
    args = [o, x, gt, w_o, ln_g, ln_b, shf, scf, wr_hl, wr_hi, br]
    aliases = {}
    if hm_rt is not None:
        ins = [pl.BlockSpec(memory_space=pl.ANY)] * 2 + ins
        args = list(hm_rt) + args
        aliases = {0: 1, 1: 2}
    return pl.pallas_call(
        functools.partial(_outln_kernel, aliased=hm_rt is not None),
        grid=(steps,),
        in_specs=ins,
        out_specs=[tok, pl.BlockSpec((tm, D // LANES, LANES), lambda i: (i + blk0, 0, 0)),
                   pl.BlockSpec((tm, LANES), lambda i: (i + blk0, 0))],
        out_shape=[jax.ShapeDtypeStruct((n, D), F32), jax.ShapeDtypeStruct((n_all, D // LANES, LANES), F32),
                   jax.ShapeDtypeStruct((n_all, LANES), F32)],
        input_output_aliases=aliases,
        compiler_params=_cp(("arbitrary",)),
        name="outln",
    )(*args)


def _store_rows(ref3, x):
    for c in range(D // LANES):
        ref3[:, c, :] = x[:, c * LANES:(c + 1) * LANES]


def _load_rows(ref3):
    return jnp.concatenate([ref3[:, c, :] for c in range(D // LANES)], axis=1)


def _gather_rows(idx_ref, n, src_hbm, dst, sem, stride=1, offset=0):
    for r in range(n):
        t = idx_ref[0, r * stride + offset]
        pltpu.make_async_copy(src_hbm.at[t], dst.at[r], sem).start()


def _wait_rows(src_hbm, dst, sem):
    n = dst.shape[0]
    pltpu.make_async_copy(src_hbm.at[pl.ds(0, n)], dst, sem).wait()


def _ffn_kernel(be_ref, tok_ref, tokn_ref, x_hbm, w1_ref, w3_ref, w2_ref, y_ref, buf, sem):
    del be_ref
    i = pl.program_id(0)
    last = pl.num_programs(0) - 1
    slot = i % 2

    @pl.when(i == 0)
    def _():
        _gather_rows(tok_ref, FFN_BLK, x_hbm, buf.at[0], sem.at[0])

    _wait_rows(x_hbm, buf.at[slot], sem.at[slot])
    _gather_rows(tokn_ref, FFN_BLK, x_hbm, buf.at[1 - slot], sem.at[1 - slot])

    xb = _load_rows(buf.at[slot]).astype(BF16)
    a = _dot(xb, w1_ref[...])
    hdn = (a * jax.nn.sigmoid(a)) * _dot(xb, w3_ref[...])
    _store_rows(y_ref, _dot(hdn.astype(BF16), w2_ref[...]))

    @pl.when(i == last)
    def _():
        _wait_rows(x_hbm, buf.at[1 - slot], sem.at[1 - slot])


def _ffn(block_expert, slot_tok, hm, w1, w3, w2):
    nblk = block_expert.shape[0]
    tok3 = slot_tok.reshape(nblk, 1, FFN_BLK)
    grid_spec = pltpu.PrefetchScalarGridSpec(
        num_scalar_prefetch=1,
        grid=(nblk,),
        in_specs=[pl.BlockSpec((None, 1, FFN_BLK), lambda i, be: (i, 0, 0), memory_space=pltpu.SMEM),
                  pl.BlockSpec((None, 1, FFN_BLK), lambda i, be: (jnp.minimum(i + 1, nblk - 1), 0, 0),
                               memory_space=pltpu.SMEM),
                  pl.BlockSpec(memory_space=pl.ANY),
                  pl.BlockSpec((None, D, D_EXPERT), lambda i, be: (be[i], 0, 0)),
                  pl.BlockSpec((None, D, D_EXPERT), lambda i, be: (be[i], 0, 0)),
                  pl.BlockSpec((None, D_EXPERT, D), lambda i, be: (be[i], 0, 0))],
        out_specs=pl.BlockSpec((FFN_BLK, D // LANES, LANES), lambda i, be: (i, 0, 0)),
        scratch_shapes=[pltpu.VMEM((2, FFN_BLK, D // LANES, LANES), F32), pltpu.SemaphoreType.DMA((2,))],
    )
    return pl.pallas_call(
        _ffn_kernel,
        grid_spec=grid_spec,
        out_shape=jax.ShapeDtypeStruct((nblk * FFN_BLK, D // LANES, LANES), F32),
        compiler_params=_cp(("arbitrary",)),
        name="moe_ffn",
    )(block_expert, tok3, tok3, hm, w1, w3, w2)


def _combine_kernel(pos_ref, posn_ref, yb_hbm, rt_ref, x_ref, gt_ref, g_ref, b_ref, o_ref, buf, sem):
    i = pl.program_id(0)
    last = pl.num_programs(0) - 1
    slot = i % 2
    tm = x_ref.shape[0]

    def issue(idx_ref, s):
        _gather_rows(idx_ref, tm, yb_hbm, buf.at[s, 0], sem.at[s], stride=2, offset=0)
        _gather_rows(idx_ref, tm, yb_hbm, buf.at[s, 1], sem.at[s], stride=2, offset=1)

    def drain(s):
        for kk in range(2):
            _wait_rows(yb_hbm, buf.at[s, kk], sem.at[s])

    @pl.when(i == 0)
    def _():
        issue(pos_ref, 0)

    drain(slot)
    issue(posn_ref, 1 - slot)

    rt = rt_ref[...]
    f = rt[:, 2:3] * _load_rows(buf.at[slot, 0]) + rt[:, 3:4] * _load_rows(buf.at[slot, 1])
    o_ref[...] = _layer_norm(ALPHA * x_ref[...] + (1.0 + gt_ref[...]) * f, g_ref[...], b_ref[...])

    @pl.when(i == last)
    def _():
        drain(1 - slot)


def _combine(pos, yb, rt, x, gt, ln_g, ln_b, per_token, tiles_per_seq):
    n = x.shape[0]
    tm = min(CMB_TM, n)
    nt = n // tm
    pos3 = pos.reshape(nt, 1, 2 * tm)
    mod = _mod_spec(per_token, tm, tiles_per_seq)
    tok = pl.BlockSpec((tm, D), lambda i: (i, 0))
    return pl.pallas_call(
        _combine_kernel,
        grid=(nt,),
        in_specs=[pl.BlockSpec((None, 1, 2 * tm), lambda i: (i, 0, 0), memory_space=pltpu.SMEM),
                  pl.BlockSpec((None, 1, 2 * tm), lambda i: (jnp.minimum(i + 1, nt - 1), 0, 0), memory_space=pltpu.SMEM),
                  pl.BlockSpec(memory_space=pl.ANY),
                  pl.BlockSpec((tm, LANES), lambda i: (i, 0)),
                  tok, mod, _full((1, D)), _full((1, D))],
        out_specs=tok,
        out_shape=jax.ShapeDtypeStruct((n, D), F32),
        scratch_shapes=[pltpu.VMEM((2, 2, tm, D // LANES, LANES), F32), pltpu.SemaphoreType.DMA((2,))],
        compiler_params=_cp(("arbitrary",)),
        name="moe_combine",
    )(pos3, pos3, yb, rt, x, gt, ln_g, ln_b)


def _dispatch(e_flat):
    n_assign = e_flat.shape[0]
    ids = jnp.arange(n_assign, dtype=I32)
    experts = jnp.arange(N_EXPERTS, dtype=I32)

    def lookup(table, e):
        return jnp.sum(jnp.where(e[:, None] == experts[None, :], table[None, :], 0), axis=1)

    e_sorted, order = lax.sort((e_flat, ids), num_keys=1, is_stable=True)
    counts = jnp.sum((e_flat[:, None] == experts[None, :]).astype(I32), axis=0)
    start = jnp.cumsum(counts) - counts
    padded = (counts + FFN_BLK - 1) // FFN_BLK * FFN_BLK
    pend = jnp.cumsum(padded)
    pstart = pend - padded
    dest = ids + lookup(pstart - start, e_sorted)
    _, pos = lax.sort((order, dest), num_keys=1)
    n_blocks = -(-(n_assign + N_EXPERTS * (FFN_BLK - 1)) // FFN_BLK)
    blk_start = jnp.arange(n_blocks, dtype=I32) * FFN_BLK
    block_expert = jnp.minimum(jnp.sum((pend[None, :] <= blk_start[:, None]).astype(I32), axis=1), N_EXPERTS - 1)
    e_slot = jnp.repeat(block_expert, FFN_BLK)
    rank = jnp.arange(n_blocks * FFN_BLK, dtype=I32) - lookup(pstart, e_slot)
    valid = rank < lookup(counts, e_slot)
    src = jnp.where(valid, rank + lookup(start, e_slot), 0)
    slot_tok = jnp.where(valid, order[src] // 2, 0)
    return block_expert, slot_tok, pos


def kernel(x_prompt, x_sample, c_prompt, c_sample, cache_k_l0, cache_v_l0, state_wkv_l1, state_shift_l1, cache_k_l2, cache_v_l2, cache_k_l3, cache_v_l3, w_ada, b_ada, ln_g, ln_b, a_w_qkv, a_sinks, a_w_o, b_mu, b_w_rkv, b_w0, b_w1, b_w2, b_a0, b_a1, b_a2, b_g1, b_g2, b_k_k, b_k_a, b_r_k, b_lnx_g, b_lnx_b, b_w_o, c_w_qkv, c_lam, c_subln_g, c_w_o, moe_w_grp, moe_b_grp, moe_w_rt, moe_b_rt, moe_w1, moe_w3, moe_w2):
    nbp, tp, _ = x_prompt.shape
    nbs, ts, _ = x_sample.shape
    n_p, n_s = nbp * tp, nbs * ts
    n_all = n_p + n_s
    tm_p = min(512, tp)
    tps_p = tp // tm_p
    tm_s = n_s
    assert n_p % tm_s == 0 and n_p % CMB_TM == 0

    xp = x_prompt.reshape(n_p, D)
    xs = x_sample.reshape(n_s, D)
    mods = _ada_all(jnp.concatenate([c_prompt, c_sample], axis=0), w_ada, b_ada)

    a_caches = [(cache_k_l0, cache_v_l0), (cache_k_l3, cache_v_l3)]
    new_states = []
    for i in range(DEPTH):
        kind, j = i % 3, i // 3
        m_p = [mods[i, :nbp, n * D:(n + 1) * D].reshape(nbp, 1, D) for n in range(6)]
        m_sb = [mods[i, nbp:, n * D:(n + 1) * D].reshape(nbs, 1, D) for n in range(6)]
        m_s = [jnp.broadcast_to(m, (nbs, ts, D)).reshape(n_s, D) for m in m_sb]

        if kind == 0:
            w = a_w_qkv[j].astype(BF16)
            splits = ((0, D), (D, D + 2 * A_KV * HD))
            q_p, kv_p = _qkv(xp, m_p[0], m_p[1], w, splits, (BF16, F32), False, tm_p, tps_p)
            q_s, kv_s = _qkv(xs, m_s[0], m_s[1], w, splits, (BF16, F32), True, tm_s, 1)
            o_p = _swa_prompt(q_p, kv_p, a_sinks[j], nbp, tp)
            o_s = _swa_sample(q_s, kv_s, a_caches[j][0], a_caches[j][1], a_sinks[j], nbs, ts)
            kw = A_KV * HD
            kv_p3 = kv_p.reshape(nbp, tp, 2 * kw)[:, tp - WINDOW:]
            kv_s3 = kv_s.reshape(nbs, ts, 2 * kw)
            k_s = jnp.concatenate([a_caches[j][0], kv_s3[..., :kw].reshape(nbs, ts, A_KV, HD)], axis=1)[:, ts:]
            v_s = jnp.concatenate([a_caches[j][1], kv_s3[..., kw:].reshape(nbs, ts, A_KV, HD)], axis=1)[:, ts:]
            new_states.append((kv_p3[..., :kw].reshape(nbp, WINDOW, A_KV, HD),
                               kv_p3[..., kw:].reshape(nbp, WINDOW, A_KV, HD), k_s, v_s))
            w_o = a_w_o[j]
        elif kind == 1:
            wts = (b_mu[j], b_w_rkv[j].astype(BF16), b_w0[j].reshape(1, D), b_w1[j].astype(BF16),
                   b_w2[j].astype(BF16), b_a0[j].reshape(1, D), b_a1[j].astype(BF16), b_a2[j].astype(BF16),
                   b_g1[j].astype(BF16), b_g2[j].astype(BF16))
            zero8 = jnp.zeros((3, D), F32)
            head_w = jnp.concatenate([b_k_k[j].reshape(1, D), b_k_a[j].reshape(1, D), b_r_k[j].reshape(1, D),
                                      b_lnx_g[j].reshape(1, D), b_lnx_b[j].reshape(1, D), zero8], axis=0)
            *rk_p, hl_p = _rwkv_proj(xp, jnp.zeros((nbp, D), F32), m_p[0], m_p[1], wts, nbp, tp)
            *rk_s, hl_s = _rwkv_proj(xs, state_shift_l1, m_sb[0], m_sb[1], wts, nbs, ts)
            s0_p = jnp.zeros((nbp, D // LANES, LANES, LANES), F32)
            o_p, st_p = _rwkv_scan(*rk_p, s0_p, head_w, nbp, tp)
            o_s, st_s = _rwkv_scan(*rk_s, _state_to_blockdiag(state_wkv_l1), head_w, nbs, ts)
            new_states.append((_blockdiag_to_state(st_p), hl_p.reshape(nbp, D),
                               _blockdiag_to_state(st_s), hl_s.reshape(nbs, D)))
            w_o = b_w_o[j]
        else:
            lam_init = 0.8 - 0.6 * math.exp(-0.3 * i)
            w = c_w_qkv[j].astype(BF16)
            splits = ((0, D), (D, 2 * D), (2 * D, 3 * D))
            q_p, k_p, v_p = _qkv(xp, m_p[0], m_p[1], w, splits, (BF16, F32, F32), False, tm_p, tps_p)
            q_s, k_s, v_s = _qkv(xs, m_s[0], m_s[1], w, splits, (BF16, F32, F32), True, tm_s, 1)
            o_p = _diff_prompt(q_p, k_p, v_p, c_lam[j], c_subln_g[j], nbp, tp, lam_init)
            o_s = _diff_sample(q_s, k_s, v_s, cache_k_l2, cache_v_l2, c_lam[j], c_subln_g[j], nbs, ts, lam_init)
            nh = D // LANES
            new_states.append((k_p.reshape(nbp, tp, nh, LANES), v_p.reshape(nbp, tp, nh, LANES),
                               k_s.reshape(nbs, ts, nh, LANES), v_s.reshape(nbs, ts, nh, LANES)))
            w_o = c_w_o[j]

        wr = jnp.concatenate([moe_w_rt[i], moe_w_grp[i], jnp.zeros((D, LANES - N_EXPERTS - N_GROUPS), F32)], axis=1)
        wr_hi = wr.astype(BF16)
        wr_lo = (wr - wr_hi.astype(F32)).astype(BF16)
        wr_hl = jnp.concatenate([wr_hi, wr_lo], axis=1)
        br = jnp.concatenate([moe_b_rt[i], moe_b_grp[i], jnp.zeros((LANES - N_EXPERTS - N_GROUPS,), F32)]).reshape(1, LANES)
        lg0, lb0 = ln_g[i, 0].reshape(1, D), ln_b[i, 0].reshape(1, D)
        lg1, lb1 = ln_g[i, 1].reshape(1, D), ln_b[i, 1].reshape(1, D)
        w_o_b = w_o.astype(BF16)

        xp, hm, rt = _outln(o_p, xp, m_p[2], w_o_b, lg0, lb0, m_p[3], m_p[4], wr_hl, wr_hi, br,
                            False, tm_p, tps_p, n_all)
        xs, hm, rt = _outln(o_s, xs, m_s[2], w_o_b, lg0, lb0, m_s[3], m_s[4], wr_hl, wr_hi, br,
                            True, tm_s, 1, n_all, hm_rt=(hm, rt))

        e_flat = rt[:, :2].astype(I32).reshape(-1)
        block_expert, slot_tok, pos = _dispatch(e_flat)
        yb = _ffn(block_expert, slot_tok, hm, moe_w1[i].astype(BF16), moe_w3[i].astype(BF16), moe_w2[i].astype(BF16))
        xp = _combine(pos[:2 * n_p], yb, rt[:n_p], xp, m_p[5], lg1, lb1, False, tp // min(CMB_TM, n_p))
        xs = _combine(pos[2 * n_p:], yb, rt[n_p:], xs, m_s[5], lg1, lb1, True, 1)

    (k0p, v0p, k0s, v0s), (wkv1p, sh1p, wkv1s, sh1s), (k2p, v2p, k2s, v2s), (k3p, v3p, k3s, v3s) = new_states
    return (xp.reshape(nbp, tp, D), xs.reshape(nbs, ts, D), k0p, v0p, k0s, v0s, wkv1p, sh1p, wkv1s, sh1s,
            k2p, v2p, k2s, v2s, k3p, v3p, k3s, v3s)
```

```python
import functools
import math

import jax
import jax.numpy as jnp
from jax import lax
from jax.experimental import pallas as pl
from jax.experimental.pallas import tpu as pltpu

F32 = jnp.float32
BF16 = jnp.bfloat16
I32 = jnp.int32

D = 1024
DEPTH = 4
LANES = 128
HD = 64

A_HEADS, A_KV = 16, 4
WINDOW, CHUNK = 128, 64
N_EXPERTS, N_GROUPS, EPG, D_EXPERT = 32, 4, 8, 256
GN_EPS = 64e-5
LN_EPS = 1e-5
NEG = -1e30
ALPHA = (2 * DEPTH) ** 0.25

FFN_BLK = 256
CMB_TM = 256
VMEM_LIMIT = 56 * 1024 * 1024


def _cp(sem):
    return pltpu.CompilerParams(dimension_semantics=sem, vmem_limit_bytes=VMEM_LIMIT)


def _dot(a, b):
    return jnp.dot(a, b, preferred_element_type=F32)


def _dot_nt(a, b):
    return lax.dot_general(a, b, (((1,), (1,)), ((), ())), preferred_element_type=F32)


def _split2(x):
    hi = x.astype(BF16)
    lo = (x - hi.astype(F32)).astype(BF16)
    return hi, lo


def _split3(x):
    hi = x.astype(BF16)
    r1 = x - hi.astype(F32)
    mid = r1.astype(BF16)
    lo = (r1 - mid.astype(F32)).astype(BF16)
    return hi, mid, lo


def _dot3(a, b, dot=_dot):
    ah, al = _split2(a)
    bh, bl = _split2(b)
    m = a.shape[0]
    if m % 16:
        return dot(ah, bh) + (dot(ah, bl) + dot(al, bh))
    top = dot(jnp.concatenate([ah, al], axis=0), bh)
    return top[:m] + (top[m:] + dot(ah, bl))


def _dot_sel(a, sel_bf16):
    h, m, l = _split3(a)
    return _dot(h, sel_bf16) + (_dot(m, sel_bf16) + _dot(l, sel_bf16))


def _layer_norm(y, g, b):
    mu = jnp.mean(y, -1, keepdims=True)
    yc = y - mu
    var = jnp.mean(yc * yc, -1, keepdims=True)
    return yc * lax.rsqrt(var + LN_EPS) * g + b


def _mod_spec(per_token, tm, tiles_per_seq):
    if per_token:
        return pl.BlockSpec((tm, D), lambda i: (i, 0))
    return pl.BlockSpec((None, 1, D), lambda i: (i // tiles_per_seq, 0, 0))


def _full(shape):
    return pl.BlockSpec(shape, lambda *_: (0,) * len(shape))


def _ada_kernel(c_ref, w_ref, b_ref, o_ref):
    c = c_ref[...]
    s = c * jax.nn.sigmoid(c)
    o_ref[...] = _dot3(s, w_ref[...]) + b_ref[...]


def _ada_all(c_all, w_ada, b_ada):
    nb = c_all.shape[0]
    tn = 1536
    return pl.pallas_call(
        _ada_kernel,
        grid=(DEPTH, 6 * D // tn),
        in_specs=[pl.BlockSpec((nb, D), lambda l, j: (0, 0)),
                  pl.BlockSpec((None, D, tn), lambda l, j: (l, 0, j)),
                  pl.BlockSpec((None, 1, tn), lambda l, j: (l, 0, j))],
        out_specs=pl.BlockSpec((None, nb, tn), lambda l, j: (l, 0, j)),
        out_shape=jax.ShapeDtypeStruct((DEPTH, nb, 6 * D), F32),
        compiler_params=_cp(("arbitrary", "arbitrary")),
        name="ada",
    )(c_all, w_ada, b_ada.reshape(DEPTH, 1, 6 * D))


def _qkv_kernel(x_ref, sh_ref, sc_ref, w_ref, *o_refs, splits):
    h = (x_ref[...] * (1.0 + sc_ref[...]) + sh_ref[...]).astype(BF16)
    for o_ref, (c0, c1) in zip(o_refs, splits):
        o_ref[...] = _dot(h, w_ref[:, c0:c1]).astype(o_ref.dtype)


def _qkv(x, sh, sc, w_bf16, splits, dtypes, per_token, tm, tiles_per_seq):
    n = x.shape[0]
    nout = w_bf16.shape[1]
    mod = _mod_spec(per_token, tm, tiles_per_seq)
    return pl.pallas_call(
        functools.partial(_qkv_kernel, splits=splits),
        grid=(n // tm,),
        in_specs=[pl.BlockSpec((tm, D), lambda i: (i, 0)), mod, mod, _full((D, nout))],
        out_specs=[pl.BlockSpec((tm, c1 - c0), lambda i: (i, 0)) for c0, c1 in splits],
        out_shape=[jax.ShapeDtypeStruct((n, c1 - c0), dt) for (c0, c1), dt in zip(splits, dtypes)],
        compiler_params=_cp(("arbitrary",)),
        name="qkv",
    )(x, sh, sc, w_bf16)


def _gqa_rows(q_slab, k_slabs, v_slabs, sinks_ref, n_invalid):
    def pair(j):
        outs = [None] * 4
        slabs = [q_slab(4 * j + a) for a in range(4)]
        nq = slabs[0].shape[0]
        half = lax.broadcasted_iota(I32, (nq, LANES), 1) // HD
        aligned = [jnp.where(half == a // 2, q, jnp.zeros_like(q)) for a, q in enumerate(slabs)]
        crossed = [jnp.where(half != a // 2, q, jnp.zeros_like(q)) for a, q in enumerate(slabs)]
        qa = jnp.concatenate(aligned, axis=0) * (HD ** -0.5)
        qc = jnp.concatenate(crossed, axis=0) * (HD ** -0.5)
        s = jnp.concatenate([_dot_nt(qa, k_slabs[j][0]), _dot_nt(qc, k_slabs[j][1])], axis=0)
        yield
        if n_invalid is not None:
            col = lax.broadcasted_iota(I32, s.shape, 1)
            s = jnp.where(col >= n_invalid, s, NEG)
        sink = jnp.concatenate(
            [jnp.full((nq, 1), sinks_ref[2 * (4 * j + g % 4) + (g % 4 // 2 if g < 4 else 1 - g % 4 // 2)], F32)
             for g in range(8)], axis=0)
        m = jnp.maximum(jnp.max(s, -1, keepdims=True), sink)
        yield
        pe = jnp.exp(s - m)
        den = jnp.sum(pe, -1, keepdims=True) + jnp.exp(sink - m)
        pv = _dot(pe.astype(BF16), v_slabs[j])
        yield
        pv = pv / den
        pa = pv[:4 * nq]
        pc = pltpu.roll(pv[4 * nq:], HD, 1)
        for a in range(4):
            ra, rc = pa[a * nq:(a + 1) * nq], pc[a * nq:(a + 1) * nq]
            outs[a] = jnp.where(half == a // 2, ra, rc)
        return outs

    return [pair(j) for j in range(A_KV // 2)]


def _kv_slabs(k_f32, v_f32):
    ks, vs = [], []
    for j in range(A_KV // 2):
        kj = k_f32[:, j * LANES:(j + 1) * LANES]
        ks.append((kj.astype(BF16), pltpu.roll(kj, HD, 1).astype(BF16)))
        vs.append(v_f32[:, j * LANES:(j + 1) * LANES].astype(BF16))
    return ks, vs


def _swa_prompt_kernel(sinks_ref, q_ref, kvp_ref, kvc_ref, o_ref, *, tq):
    i = pl.program_id(1)
    kv = jnp.concatenate([kvp_ref[...], kvc_ref[...]], axis=0)
    kw = A_KV * HD
    ks, vs = _kv_slabs(kv[:, :kw], kv[:, kw:])
    gens = []
    for c in range(tq // CHUNK):
        r0, r1 = c * CHUNK, c * CHUNK + WINDOW + CHUNK
        kc = [(a[r0:r1], b[r0:r1]) for a, b in ks]
        vc = [a[r0:r1] for a in vs]
        n_bad = max(WINDOW - c * CHUNK, 0)
        n_invalid = jnp.where(i == 0, n_bad, 0) if n_bad else None
        gens += _gqa_rows(lambda p, c=c: q_ref[c * CHUNK:(c + 1) * CHUNK, p * LANES:(p + 1) * LANES],
                          kc, vc, sinks_ref, n_invalid)
    for n, outs in enumerate(_lockstep(gens)):
        c, j = n // 2, n % 2
        for a, o in enumerate(outs):
            p = 4 * j + a
            o_ref[c * CHUNK:(c + 1) * CHUNK, p * LANES:(p + 1) * LANES] = o.astype(o_ref.dtype)


def _swa_prompt(q, kv, sinks, nb, t):
    tq = min(256, t)
    tps = t // tq
    wb = tq // WINDOW
    return pl.pallas_call(
        functools.partial(_swa_prompt_kernel, tq=tq),
        grid=(nb, tps),
        in_specs=[pl.BlockSpec(memory_space=pltpu.SMEM),
                  pl.BlockSpec((tq, D), lambda b, i: (b * tps + i, 0)),
                  pl.BlockSpec((WINDOW, 2 * A_KV * HD), lambda b, i: (jnp.maximum((b * tps + i) * wb - 1, 0), 0)),
                  pl.BlockSpec((tq, 2 * A_KV * HD), lambda b, i: (b * tps + i, 0))],
        out_specs=pl.BlockSpec((tq, D), lambda b, i: (b * tps + i, 0)),
        out_shape=jax.ShapeDtypeStruct((nb * t, D), BF16),
        compiler_params=_cp(("arbitrary", "arbitrary")),
        name="swa_prompt",
    )(sinks, q, kv, kv)


def _swa_sample_kernel(sinks_ref, q_ref, ck_ref, cv_ref, kvn_ref, o_ref):
    kw = A_KV * HD
    kvn = kvn_ref[...]
    k_all = jnp.concatenate([ck_ref[...], kvn[:, :kw]], axis=0)
    v_all = jnp.concatenate([cv_ref[...], kvn[:, kw:]], axis=0)
    ks, vs = _kv_slabs(k_all, v_all)
    pairs = _lockstep(_gqa_rows(lambda p: q_ref[:, p * LANES:(p + 1) * LANES], ks, vs, sinks_ref, None))
    for p, o in enumerate(pairs[0] + pairs[1]):
        o_ref[:, p * LANES:(p + 1) * LANES] = o.astype(o_ref.dtype)


def _swa_sample(q, kv, cache_k, cache_v, sinks, nb, s):
    kw = A_KV * HD
    return pl.pallas_call(
        _swa_sample_kernel,
        grid=(nb,),
        in_specs=[pl.BlockSpec(memory_space=pltpu.SMEM),
                  pl.BlockSpec((s, D), lambda b: (b, 0)),
                  pl.BlockSpec((None, WINDOW, kw), lambda b: (b, 0, 0)),
                  pl.BlockSpec((None, WINDOW, kw), lambda b: (b, 0, 0)),
                  pl.BlockSpec((s, 2 * kw), lambda b: (b, 0))],
        out_specs=pl.BlockSpec((s, D), lambda b: (b, 0)),
        out_shape=jax.ShapeDtypeStruct((nb * s, D), BF16),
        compiler_params=_cp(("arbitrary",)),
        name="swa_sample",
    )(sinks, q, cache_k.reshape(nb, WINDOW, kw), cache_v.reshape(nb, WINDOW, kw), kv)


def _diff_lambda(lam_ref, lam_init):
    l = lam_ref[...]
    return (jnp.exp(jnp.sum(l[0:1] * l[1:2], -1, keepdims=True))
            - jnp.exp(jnp.sum(l[2:3] * l[3:4], -1, keepdims=True)) + lam_init)


def _stack_halves(q):
    half = lax.broadcasted_iota(I32, q.shape, 1) // HD
    z = jnp.zeros_like(q)
    return jnp.concatenate([jnp.where(half == 0, q, z), jnp.where(half == 1, q, z)], axis=0)


def _diff_finish(o2, n, lam_full, sg, lam_init):
    o = o2[:n] - lam_full * o2[n:]
    return o * lax.rsqrt(jnp.mean(o * o, -1, keepdims=True) + LN_EPS) * sg * (1.0 - lam_init)


def _diff_prompt_kernel(lam_ref, q_ref, k_ref, v_ref, sg_ref, o_ref, kb, vtb, *, tq, lam_init):
    i = pl.program_id(2)
    nhs = kb.shape[0]

    @pl.when(i == 0)
    def _():
        for hh in range(nhs):
            lanes = slice(hh * LANES, (hh + 1) * LANES)
            for c in range(kb.shape[1]):
                kb[hh, c] = k_ref[c * tq:(c + 1) * tq, lanes].astype(BF16)
                vtb[hh, c] = v_ref[c * tq:(c + 1) * tq, lanes].T.astype(BF16)

    qs = [_stack_halves(q_ref[:, hh * LANES:(hh + 1) * LANES]) * (HD ** -0.5) for hh in range(nhs)]

    def head_step(hh, j, carry, masked):
        m, l, acc = carry
        st = _dot_nt(kb[hh, j], qs[hh])
        yield
        if masked:
            kc = lax.broadcasted_iota(I32, st.shape, 0) // CHUNK
            qc = (lax.broadcasted_iota(I32, st.shape, 1) % tq) // CHUNK
            st = jnp.where(kc <= qc, st, NEG)
        m_new = jnp.maximum(m, jnp.max(st, 0, keepdims=True))
        yield
        p = jnp.exp(st - m_new)
        pv = _dot(vtb[hh, j], p.astype(BF16))
        yield
        a = jnp.exp(m - m_new)
        return m_new, a * l + jnp.sum(p, 0, keepdims=True), a * acc + pv

    def step(j, carries, masked):
        return tuple(_lockstep([head_step(hh, j, carries[hh], masked) for hh in range(nhs)]))

    init = (jnp.full((1, 2 * tq), NEG, F32), jnp.zeros((1, 2 * tq), F32), jnp.zeros((LANES, 2 * tq), F32))
    carries = lax.fori_loop(0, i, lambda j, c: step(j, c, False), (init,) * nhs)
    lam = _diff_lambda(lam_ref, lam_init)
    for hh, (_, l, acc) in enumerate(step(i, carries, True)):
        o2t = acc / l
        o = (o2t[:, :tq] - lam * o2t[:, tq:]).T
        o = o * lax.rsqrt(jnp.mean(o * o, -1, keepdims=True) + LN_EPS) * sg_ref[...] * (1.0 - lam_init)
        o_ref[:, hh * LANES:(hh + 1) * LANES] = o.astype(o_ref.dtype)


def _diff_prompt(q, k, v, lam, subln_g, nb, t, lam_init):
    tq = min(256, t)
    nhs = 4
    w = nhs * LANES
    q3, k3, v3 = (z.reshape(nb, t, D) for z in (q, k, v))
    return pl.pallas_call(
        functools.partial(_diff_prompt_kernel, tq=tq, lam_init=lam_init),
        grid=(nb, D // w, t // tq),
        in_specs=[_full((4, HD)),
                  pl.BlockSpec((None, tq, w), lambda b, h, i: (b, i, h)),
                  pl.BlockSpec((None, t, w), lambda b, h, i: (b, 0, h)),
                  pl.BlockSpec((None, t, w), lambda b, h, i: (b, 0, h)),
                  _full((1, LANES))],
        out_specs=pl.BlockSpec((None, tq, w), lambda b, h, i: (b, i, h)),
        out_shape=jax.ShapeDtypeStruct((nb, t, D), BF16),
        scratch_shapes=[pltpu.VMEM((nhs, t // tq, tq, LANES), BF16), pltpu.VMEM((nhs, t // tq, LANES, tq), BF16)],
        compiler_params=_cp(("arbitrary", "arbitrary", "arbitrary")),
        name="diff_prompt",
    )(lam, q3, k3, v3, subln_g.reshape(1, LANES)).reshape(nb * t, D)


def _diff_sample_kernel(lam_ref, q_ref, kn_ref, vn_ref, ck_ref, cv_ref, sg_ref, o_ref, *, lam_init):
    n = q_ref.shape[0]
    qs = _stack_halves(q_ref[...])
    s1 = _dot_nt(qs, ck_ref[...].astype(BF16)) * (HD ** -0.5)
    s2 = _dot_nt(qs, kn_ref[...].astype(BF16)) * (HD ** -0.5)
    m = jnp.maximum(jnp.max(s1, -1, keepdims=True), jnp.max(s2, -1, keepdims=True))
    p1 = jnp.exp(s1 - m)
    p2 = jnp.exp(s2 - m)
    l = jnp.sum(p1, -1, keepdims=True) + jnp.sum(p2, -1, keepdims=True)
    acc = _dot(p1.astype(BF16), cv_ref[...].astype(BF16)) + _dot(p2.astype(BF16), vn_ref[...].astype(BF16))
    o = _diff_finish(acc / l, n, _diff_lambda(lam_ref, lam_init), sg_ref[...], lam_init)
    o_ref[...] = o.astype(o_ref.dtype)


def _diff_sample(q, k, v, cache_k, cache_v, lam, subln_g, nb, s, lam_init):
    nh = D // LANES
    past = cache_k.shape[1]
    q3, k3, v3 = (z.reshape(nb, s, D) for z in (q, k, v))
    new = pl.BlockSpec((None, s, LANES), lambda b, h: (b, 0, h))
    old = pl.BlockSpec((None, past, LANES), lambda b, h: (b, 0, h))
    return pl.pallas_call(
        functools.partial(_diff_sample_kernel, lam_init=lam_init),
        grid=(nb, nh),
        in_specs=[_full((4, HD)), new, new, new, old, old, _full((1, LANES))],
        out_specs=new,
        out_shape=jax.ShapeDtypeStruct((nb, s, D), BF16),
        compiler_params=_cp(("arbitrary", "arbitrary")),
        name="diff_sample",
    )(lam, q3, k3, v3, cache_k.reshape(nb, past, D), cache_v.reshape(nb, past, D),
      subln_g.reshape(1, LANES)).reshape(nb * s, D)


def _rwkv_proj_kernel(x_ref, xp_ref, hp_ref, sh_ref, sc_ref, mu_ref, wrkv_ref, w0_ref, w1_ref, w2_ref,
                      a0_ref, a1_ref, a2_ref, g1_ref, g2_ref,
                      r_ref, k_ref, v_ref, lw_ref, a_ref, g_ref, hl_ref):
    i = pl.program_id(1)
    sc1 = 1.0 + sc_ref[...]
    sh = sh_ref[...]
    h = x_ref[...] * sc1 + sh
    prev_in_seq = xp_ref[7:8, :] * sc1 + sh
    prev = jnp.where(i == 0, hp_ref[...], prev_in_seq)
    row = lax.broadcasted_iota(I32, h.shape, 0)
    hs = jnp.where(row == 0, prev, pltpu.roll(h, 1, 0))
    xx = hs - h
    mu = mu_ref[...]

    def mix(n):
        return (h + xx * mu[n:n + 1]).astype(BF16)

    r_ref[...] = _dot(mix(0), wrkv_ref[0])
    k_ref[...] = _dot(mix(2), wrkv_ref[1])
    v_ref[...] = _dot(mix(3), wrkv_ref[2])
    z = w0_ref[...] + _dot(jnp.tanh(_dot(mix(1), w1_ref[...])).astype(BF16), w2_ref[...])
    softplus_neg = jnp.maximum(-z, 0.0) + jnp.log(1.0 + jnp.exp(-jnp.abs(z)))
    lw_ref[...] = -jnp.exp(-softplus_neg - 0.5)
    a_ref[...] = jax.nn.sigmoid(a0_ref[...] + _dot(_dot(mix(4), a1_ref[...]).astype(BF16), a2_ref[...]))
    g_ref[...] = _dot(jax.nn.sigmoid(_dot(mix(5), g1_ref[...])).astype(BF16), g2_ref[...])
    hl_ref[...] = h[h.shape[0] - 1:, :]


def _rwkv_proj(x, h_prev, sh, sc, wts, nb, t):
    tm = min(512, t)
    tps = t // tm
    x3 = x.reshape(nb, t, D)
    tok = pl.BlockSpec((None, tm, D), lambda b, i: (b, i, 0))
    per_b = pl.BlockSpec((None, 1, D), lambda b, i: (b, 0, 0))
    ins = [tok,
           pl.BlockSpec((None, 8, D), lambda b, i: (b, jnp.maximum(i * (tm // 8) - 1, 0), 0)),
           per_b, per_b, per_b] + [_full(w.shape) for w in wts]
    outs = pl.pallas_call(
        _rwkv_proj_kernel,
        grid=(nb, tps),
        in_specs=ins,
        out_specs=[tok] * 6 + [per_b],
        out_shape=[jax.ShapeDtypeStruct((nb, t, D), F32)] * 6 + [jax.ShapeDtypeStruct((nb, 1, D), F32)],
        compiler_params=_cp(("arbitrary", "arbitrary")),
        name="rwkv_proj",
    )(x3, x3, h_prev.reshape(nb, 1, D), sh, sc, *wts)
    return outs


def _stack(x):
    half = lax.broadcasted_iota(I32, x.shape, 1) // HD
    z = jnp.zeros_like(x)
    return jnp.concatenate([jnp.where(half == 0, x, z), jnp.where(half == 1, x, z)], axis=0)


def _unstack(x2):
    n = x2.shape[0] // 2
    return x2[:n] + x2[n:]


def _rwkv_prep_chunk(r, k, v, lw, a, kk_w, ka_w, rk_w, bd, tri):
    L = r.shape[0]
    n2 = 2 * L

    def seg(x):
        return _dot_sel(x, bd)

    stack = _stack
    kkr = k * kk_w
    kk = kkr / jnp.maximum(jnp.sqrt(seg(kkr * kkr)), 1e-12)
    kp = k * (1.0 + (a - 1.0) * ka_w)
    cum = _dot_sel_lhs(tri, lw)
    yield
    g_in = jnp.exp(cum)
    g_ex = jnp.exp(cum - lw)
    g_inv = jnp.exp(-cum)
    g_last = g_in[L - 1:L, :]
    at = kk * g_ex
    bt = kk * a * g_inv
    kt = kp * g_inv
    rt = r * g_in
    a_s, r_s = stack(at), stack(rt)
    ar = jnp.concatenate([a_s, r_s], axis=0)
    bk = jnp.concatenate([stack(bt), stack(kt)], axis=0)
    vs = stack(v)
    pair = _dot3(ar, bk, _dot_nt)
    yield
    row = lax.broadcasted_iota(I32, (n2, n2), 0)
    col = lax.broadcasted_iota(I32, (n2, n2), 1)
    strict = col < row
    incl = col <= row
    zero = jnp.zeros((n2, n2), F32)
    x = jnp.where(strict, -pair[:n2, :n2], zero)
    tinv = jnp.where(row == col, 1.0, 0.0).astype(F32) + x
    x = _dot3(x, x)
    kv = _dot3(jnp.concatenate([jnp.where(strict, pair[:n2, n2:], zero),
                                jnp.where(incl, pair[n2:, n2:], zero)], axis=0), vs)
    yield
    for _ in range(int(math.log2(L)) - 2):
        both = _dot3(jnp.concatenate([tinv, x], axis=0), x)
        tinv = tinv + both[:n2]
        x = both[n2:]
        yield
    tinv = tinv + _dot3(tinv, x)
    c_vk = _dot3(vs.T, stack(kt * g_last))
    bonus = seg(r * kp * rk_w) * v
    yield
    wu = _dot3(tinv, jnp.concatenate([a_s, kv[:n2]], axis=1))
    yield
    mw = _dot3(jnp.where(incl, -pair[n2:, :n2], zero), wu)
    gb = _dot3(wu.T, stack(bt * g_last))
    yield
    w2 = r_s + mw[:, :LANES]
    y0 = mw[:, LANES:] + kv[n2:]
    rs = lax.broadcasted_iota(I32, (LANES, LANES), 0)
    cs = lax.broadcasted_iota(I32, (LANES, LANES), 1)
    g_mat = jnp.where(rs == cs, g_last, 0.0) - gb[:LANES]
    c_mat = c_vk - gb[LANES:]
    return _unstack(w2), _unstack(y0), bonus, _unstack(g_mat), _unstack(c_mat)


def _lockstep(gens):
    results = [None] * len(gens)
    live = list(range(len(gens)))
    while live:
        still = []
        for n in live:
            try:
                next(gens[n])
                still.append(n)
            except StopIteration as stop:
                results[n] = stop.value
        live = still
    return results


def _dot_sel_lhs(sel_bf16, b):
    h, m, l = _split3(b)
    return _dot(sel_bf16, h) + (_dot(sel_bf16, m) + _dot(sel_bf16, l))


def _block_diag_ones():
    rr = lax.broadcasted_iota(I32, (LANES, LANES), 0)
    cc = lax.broadcasted_iota(I32, (LANES, LANES), 1)
    return jnp.where(rr // HD == cc // HD, 1.0, 0.0).astype(BF16)


def _rwkv_prep_kernel(r_ref, k_ref, v_ref, lw_ref, a_ref, hw_ref, w2_ref, y0_ref, bn_ref, gc_ref, cc_ref,
                      *, L, nchunk, nslab):
    bd = _block_diag_ones()
    rl = lax.broadcasted_iota(I32, (L, L), 0)
    cl = lax.broadcasted_iota(I32, (L, L), 1)
    tri = jnp.where(cl <= rl, 1.0, 0.0).astype(BF16)
    where, gens = [], []
    for c in range(nchunk):
        rows = slice(c * L, (c + 1) * L)
        for s in range(nslab):
            lanes = slice(s * LANES, (s + 1) * LANES)
            hw = hw_ref[:, lanes]
            where.append((rows, slice(c * HD, (c + 1) * HD), lanes))
            gens.append(_rwkv_prep_chunk(
                r_ref[rows, lanes], k_ref[rows, lanes], v_ref[rows, lanes], lw_ref[rows, lanes], a_ref[rows, lanes],
                hw[0:1], hw[1:2], hw[2:3], bd, tri))
    for (rows, srows, lanes), (w2, y0, bonus, gc, cc) in zip(where, _lockstep(gens)):
        w2_ref[rows, lanes] = w2
        y0_ref[rows, lanes] = y0
        bn_ref[rows, lanes] = bonus
        gc_ref[srows, lanes] = gc
        cc_ref[srows, lanes] = cc


def _rwkv_scan_kernel(w2_ref, y0_ref, bn_ref, g_ref, gc_ref, cc_ref, s0_ref, hw_ref, y_ref, sT_ref, s_scr, *, L):
    i = pl.program_id(1)
    tt = w2_ref.shape[0]
    nslab = D // LANES

    @pl.when(i == 0)
    def _():
        s_scr[...] = s0_ref[...]

    bd = _block_diag_ones()

    def chunk(c, carry):
        rows = pl.ds(pl.multiple_of(c * L, L), L)
        srows = pl.ds(pl.multiple_of(c * HD, HD), HD)

        def slab(s):
            lanes = slice(s * LANES, (s + 1) * LANES)
            hw = hw_ref[:, lanes]
            st = s_scr[s]
            ys = _dot3(_stack(w2_ref[rows, lanes]), st, _dot_nt)
            s_scr[s] = _dot3(st, _stack(gc_ref[srows, lanes])) + _stack(cc_ref[srows, lanes])
            yield
            y = _unstack(ys) + y0_ref[rows, lanes]
            ym = _dot_sel(y, bd) * (1.0 / HD)
            yield
            yc = y - ym
            yv = _dot_sel(yc * yc, bd) * (1.0 / HD)
            yield
            yn = yc * lax.rsqrt(yv + GN_EPS) * hw[3:4] + hw[4:5]
            y_ref[rows, lanes] = ((yn + bn_ref[rows, lanes]) * g_ref[rows, lanes]).astype(y_ref.dtype)

        _lockstep([slab(s) for s in range(nslab)])
        return carry

    lax.fori_loop(0, tt // L, chunk, 0)

    @pl.when(i == pl.num_programs(1) - 1)
    def _():
        sT_ref[...] = s_scr[...]


def _rwkv_scan(r, k, v, lw, a, g, s0_bd, head_w, nb, t):
    L = min(64, t)
    nchunk = min(2, t // L)
    nslab = 4
    tt1 = nchunk * L
    w = nslab * LANES
    tok1 = pl.BlockSpec((None, tt1, w), lambda b, sg, i: (b, i, sg))
    cmp1 = pl.BlockSpec((None, nchunk * HD, w), lambda b, sg, i: (b, i, sg))
    ns = (t // L) * HD
    w2, y0, bonus, gc, cc = pl.pallas_call(
        functools.partial(_rwkv_prep_kernel, L=L, nchunk=nchunk, nslab=nslab),
        grid=(nb, D // w, t // tt1),
        in_specs=[tok1] * 5 + [pl.BlockSpec((8, w), lambda b, sg, i: (0, sg))],
        out_specs=[tok1] * 3 + [cmp1] * 2,
        out_shape=[jax.ShapeDtypeStruct((nb, t, D), F32)] * 3 + [jax.ShapeDtypeStruct((nb, ns, D), F32)] * 2,
        compiler_params=_cp(("arbitrary", "arbitrary", "arbitrary")),
        name="rwkv_prep",
    )(r, k, v, lw, a, head_w)

    tt = min(256, t)
    tok = pl.BlockSpec((None, tt, D), lambda b, i: (b, i, 0))
    cmp = pl.BlockSpec((None, (tt // L) * HD, D), lambda b, i: (b, i, 0))
    st = pl.BlockSpec((None, D // LANES, LANES, LANES), lambda b, i: (b, 0, 0, 0))
    y, s_t = pl.pallas_call(
        functools.partial(_rwkv_scan_kernel, L=L),
        grid=(nb, t // tt),
        in_specs=[tok] * 4 + [cmp] * 2 + [st, _full((8, D))],
        out_specs=[tok, st],
        out_shape=[jax.ShapeDtypeStruct((nb, t, D), BF16),
                   jax.ShapeDtypeStruct((nb, D // LANES, LANES, LANES), F32)],
        scratch_shapes=[pltpu.VMEM((D // LANES, LANES, LANES), F32)],
        compiler_params=_cp(("arbitrary", "arbitrary")),
        name="rwkv_scan",
    )(w2, y0, bonus, g, gc, cc, s0_bd, head_w)
    return y.reshape(nb * t, D), s_t


def _state_to_blockdiag(s):
    nb = s.shape[0]
    s = s.astype(F32).reshape(nb, D // LANES, 2, HD, HD)
    z = jnp.zeros_like(s[:, :, 0])
    top = jnp.concatenate([s[:, :, 0], z], axis=-1)
    bot = jnp.concatenate([z, s[:, :, 1]], axis=-1)
    return jnp.concatenate([top, bot], axis=-2)


def _blockdiag_to_state(sb):
    nb = sb.shape[0]
    s0 = sb[:, :, :HD, :HD]
    s1 = sb[:, :, HD:, HD:]
    return jnp.stack([s0, s1], axis=2).reshape(nb, D // HD, HD, HD)


def _route(h, wr_hl_ref, wr_hi_ref, br_ref):
    hh, hl = _split2(h)
    both = _dot(hh, wr_hl_ref[...])
    lg = both[:, :LANES] + both[:, LANES:] + _dot(hl, wr_hi_ref[...]) + br_ref[...]
    lane = lax.broadcasted_iota(I32, lg.shape, 1)
    lane_f = lane.astype(F32)
    big = 1e9
    is_g = (lane >= N_EXPERTS) & (lane < N_EXPERTS + N_GROUPS)
    gl = jnp.where(is_g, lg, -jnp.inf)
    gmax = jnp.max(gl, -1, keepdims=True)
    gidx = jnp.min(jnp.where(gl == gmax, lane_f - N_EXPERTS, big), -1, keepdims=True)
    gw = 1.0 / jnp.sum(jnp.exp(gl - gmax), -1, keepdims=True)
    in_grp = (lane < N_EXPERTS) & ((lane // EPG).astype(F32) == gidx)
    el = jnp.where(in_grp, lg, -jnp.inf)
    m1 = jnp.max(el, -1, keepdims=True)
    i1 = jnp.min(jnp.where(el == m1, lane_f, big), -1, keepdims=True)
    el2 = jnp.where(lane_f == i1, -jnp.inf, el)
    m2 = jnp.max(el2, -1, keepdims=True)
    i2 = jnp.min(jnp.where(el2 == m2, lane_f, big), -1, keepdims=True)
    e2 = jnp.exp(m2 - m1)
    g1 = gw / (1.0 + e2)
    g2 = gw * e2 / (1.0 + e2)
    return jnp.where(lane == 0, i1, jnp.where(lane == 1, i2, jnp.where(lane == 2, g1, jnp.where(lane == 3, g2, 0.0))))


def _outln_kernel(*refs, aliased):
    if aliased:
        refs = refs[2:]
    (o_ref, x_ref, gt_ref, w_ref, g_ref, b_ref, shf_ref, scf_ref, wr_hl_ref, wr_hi_ref, br_ref,
     xn_ref, hm_ref, rt_ref) = refs
    op = _dot(o_ref[...], w_ref[...])
    xn = _layer_norm(ALPHA * x_ref[...] + (1.0 + gt_ref[...]) * op, g_ref[...], b_ref[...])
    xn_ref[...] = xn
    h = xn * (1.0 + scf_ref[...]) + shf_ref[...]
    _store_rows(hm_ref, h)
    rt_ref[...] = _route(h, wr_hl_ref, wr_hi_ref, br_ref)


def _outln(o, x, gt, w_o, ln_g, ln_b, shf, scf, wr_hl, wr_hi, br, per_token, tm, tiles_per_seq, n_all, hm_rt=None):
    n = x.shape[0]
    nt = n // tm
    blk0 = 0 if hm_rt is None else (n_all - n) // tm
    steps = nt if hm_rt is not None else n_all // tm
    if per_token:
        mod = pl.BlockSpec((tm, D), lambda i: (jnp.minimum(i, nt - 1), 0))
    else:
        mod = pl.BlockSpec((None, 1, D), lambda i: (jnp.minimum(i, nt - 1) // tiles_per_seq, 0, 0))
    tok = pl.BlockSpec((tm, D), lambda i: (jnp.minimum(i, nt - 1), 0))
    ins = [tok, tok, mod, _full((D, D)), _full((1, D)), _full((1, D)), mod, mod,
           _full((D, 2 * LANES)), _full((D, LANES)), _full((1, LANES))]
    args = [o, x, gt, w_o, ln_g, ln_b, shf, scf, wr_hl, wr_hi, br]
    aliases = {}
    if hm_rt is not None:
        ins = [pl.BlockSpec(memory_space=pl.ANY)] * 2 + ins
        args = list(hm_rt) + args
        aliases = {0: 1, 1: 2}
    return pl.pallas_call(
        functools.partial(_outln_kernel, aliased=hm_rt is not None),
        grid=(steps,),
        in_specs=ins,
        out_specs=[tok, pl.BlockSpec((tm * ROW_TILE, LANES), lambda i: (i + blk0, 0)),
                   pl.BlockSpec((tm, LANES), lambda i: (i + blk0, 0))],
        out_shape=[jax.ShapeDtypeStruct((n, D), F32), jax.ShapeDtypeStruct((n_all * ROW_TILE, LANES), F32),
                   jax.ShapeDtypeStruct((n_all, LANES), F32)],
        input_output_aliases=aliases,
        compiler_params=_cp(("arbitrary",)),
        name="outln",
    )(*args)


ROW_TILE = D // LANES


def _store_rows(ref2, x):
    n = x.shape[0]
    for c in range(ROW_TILE):
        ref2[pl.ds(c, n, stride=ROW_TILE), :] = x[:, c * LANES:(c + 1) * LANES]


def _load_rows(ref2):
    n = ref2.shape[0] // ROW_TILE
    return jnp.concatenate([ref2[pl.ds(c, n, stride=ROW_TILE), :] for c in range(ROW_TILE)], axis=1)


def _gather_rows(idx_ref, n, src_hbm, dst, sem, stride=1, offset=0):
    for r in range(n):
        t = pl.multiple_of(idx_ref[0, r * stride + offset] * ROW_TILE, ROW_TILE)
        pltpu.make_async_copy(src_hbm.at[pl.ds(t, ROW_TILE)], dst.at[pl.ds(r * ROW_TILE, ROW_TILE)], sem).start()


def _wait_rows(src_hbm, dst, sem):
    n = dst.shape[0]
    pltpu.make_async_copy(src_hbm.at[pl.ds(0, n)], dst, sem).wait()


def _ffn_kernel(be_ref, tok_ref, tokn_ref, x_hbm, w1_ref, w3_ref, w2_ref, y_ref, buf, sem):
    del be_ref
    i = pl.program_id(0)
    last = pl.num_programs(0) - 1
    slot = i % 2

    @pl.when(i == 0)
    def _():
        _gather_rows(tok_ref, FFN_BLK, x_hbm, buf.at[0], sem.at[0])

    _wait_rows(x_hbm, buf.at[slot], sem.at[slot])
    _gather_rows(tokn_ref, FFN_BLK, x_hbm, buf.at[1 - slot], sem.at[1 - slot])

    xb = _load_rows(buf.at[slot]).astype(BF16)
    a = _dot(xb, w1_ref[...])
    hdn = (a * jax.nn.sigmoid(a)) * _dot(xb, w3_ref[...])
    _store_rows(y_ref, _dot(hdn.astype(BF16), w2_ref[...]))

    @pl.when(i == last)
    def _():
        _wait_rows(x_hbm, buf.at[1 - slot], sem.at[1 - slot])


def _ffn(block_expert, slot_tok, hm, w1, w3, w2):
    nblk = block_expert.shape[0]
    tok3 = slot_tok.reshape(nblk, 1, FFN_BLK)
    grid_spec = pltpu.PrefetchScalarGridSpec(
        num_scalar_prefetch=1,
        grid=(nblk,),
        in_specs=[pl.BlockSpec((None, 1, FFN_BLK), lambda i, be: (i, 0, 0), memory_space=pltpu.SMEM),
                  pl.BlockSpec((None, 1, FFN_BLK), lambda i, be: (jnp.minimum(i + 1, nblk - 1), 0, 0),
                               memory_space=pltpu.SMEM),
                  pl.BlockSpec(memory_space=pl.ANY),
                  pl.BlockSpec((None, D, D_EXPERT), lambda i, be: (be[i], 0, 0)),
                  pl.BlockSpec((None, D, D_EXPERT), lambda i, be: (be[i], 0, 0)),
                  pl.BlockSpec((None, D_EXPERT, D), lambda i, be: (be[i], 0, 0))],
        out_specs=pl.BlockSpec((FFN_BLK * ROW_TILE, LANES), lambda i, be: (i, 0)),
        scratch_shapes=[pltpu.VMEM((2, FFN_BLK * ROW_TILE, LANES), F32), pltpu.SemaphoreType.DMA((2,))],
    )
    return pl.pallas_call(
        _ffn_kernel,
        grid_spec=grid_spec,
        out_shape=jax.ShapeDtypeStruct((nblk * FFN_BLK * ROW_TILE, LANES), F32),
        compiler_params=_cp(("arbitrary",)),
        name="moe_ffn",
    )(block_expert, tok3, tok3, hm, w1, w3, w2)


def _combine_kernel(pos_ref, posn_ref, yb_hbm, rt_ref, x_ref, gt_ref, g_ref, b_ref, o_ref, buf, sem):
    i = pl.program_id(0)
    last = pl.num_programs(0) - 1
    slot = i % 2
    tm = x_ref.shape[0]

    def issue(idx_ref, s):
        _gather_rows(idx_ref, tm, yb_hbm, buf.at[s, 0], sem.at[s], stride=2, offset=0)
        _gather_rows(idx_ref, tm, yb_hbm, buf.at[s, 1], sem.at[s], stride=2, offset=1)

    def drain(s):
        for kk in range(2):
            _wait_rows(yb_hbm, buf.at[s, kk], sem.at[s])

    @pl.when(i == 0)
    def _():
        issue(pos_ref, 0)

    drain(slot)
    issue(posn_ref, 1 - slot)

    rt = rt_ref[...]
    f = rt[:, 2:3] * _load_rows(buf.at[slot, 0]) + rt[:, 3:4] * _load_rows(buf.at[slot, 1])
    o_ref[...] = _layer_norm(ALPHA * x_ref[...] + (1.0 + gt_ref[...]) * f, g_ref[...], b_ref[...])

    @pl.when(i == last)
    def _():
        drain(1 - slot)


def _combine(pos, yb, rt, x, gt, ln_g, ln_b, per_token, tiles_per_seq):
    n = x.shape[0]
    tm = min(CMB_TM, n)
    nt = n // tm
    pos3 = pos.reshape(nt, 1, 2 * tm)
    mod = _mod_spec(per_token, tm, tiles_per_seq)
    tok = pl.BlockSpec((tm, D), lambda i: (i, 0))
    return pl.pallas_call(
        _combine_kernel,
        grid=(nt,),
        in_specs=[pl.BlockSpec((None, 1, 2 * tm), lambda i: (i, 0, 0), memory_space=pltpu.SMEM),
                  pl.BlockSpec((None, 1, 2 * tm), lambda i: (jnp.minimum(i + 1, nt - 1), 0, 0), memory_space=pltpu.SMEM),
                  pl.BlockSpec(memory_space=pl.ANY),
                  pl.BlockSpec((tm, LANES), lambda i: (i, 0)),
                  tok, mod, _full((1, D)), _full((1, D))],
        out_specs=tok,
        out_shape=jax.ShapeDtypeStruct((n, D), F32),
        scratch_shapes=[pltpu.VMEM((2, 2, tm * ROW_TILE, LANES), F32), pltpu.SemaphoreType.DMA((2,))],
        compiler_params=_cp(("arbitrary",)),
        name="moe_combine",
    )(pos3, pos3, yb, rt, x, gt, ln_g, ln_b)


def _dispatch(e_flat):
    n_assign = e_flat.shape[0]
    ids = jnp.arange(n_assign, dtype=I32)
    experts = jnp.arange(N_EXPERTS, dtype=I32)

    def lookup(table, e):
        return jnp.sum(jnp.where(e[:, None] == experts[None, :], table[None, :], 0), axis=1)

    e_sorted, order = lax.sort((e_flat, ids), num_keys=1, is_stable=True)
    counts = jnp.sum((e_flat[:, None] == experts[None, :]).astype(I32), axis=0)
    start = jnp.cumsum(counts) - counts
    padded = (counts + FFN_BLK - 1) // FFN_BLK * FFN_BLK
    pend = jnp.cumsum(padded)
    pstart = pend - padded
    dest = ids + lookup(pstart - start, e_sorted)
    _, pos = lax.sort((order, dest), num_keys=1)
    n_blocks = -(-(n_assign + N_EXPERTS * (FFN_BLK - 1)) // FFN_BLK)
    blk_start = jnp.arange(n_blocks, dtype=I32) * FFN_BLK
    block_expert = jnp.minimum(jnp.sum((pend[None, :] <= blk_start[:, None]).astype(I32), axis=1), N_EXPERTS - 1)
    e_slot = jnp.repeat(block_expert, FFN_BLK)
    rank = jnp.arange(n_blocks * FFN_BLK, dtype=I32) - lookup(pstart, e_slot)
    valid = rank < lookup(counts, e_slot)
    src = jnp.where(valid, rank + lookup(start, e_slot), 0)
    slot_tok = jnp.where(valid, order[src] // 2, 0)
    return block_expert, slot_tok, pos


def kernel(x_prompt, x_sample, c_prompt, c_sample, cache_k_l0, cache_v_l0, state_wkv_l1, state_shift_l1, cache_k_l2, cache_v_l2, cache_k_l3, cache_v_l3, w_ada, b_ada, ln_g, ln_b, a_w_qkv, a_sinks, a_w_o, b_mu, b_w_rkv, b_w0, b_w1, b_w2, b_a0, b_a1, b_a2, b_g1, b_g2, b_k_k, b_k_a, b_r_k, b_lnx_g, b_lnx_b, b_w_o, c_w_qkv, c_lam, c_subln_g, c_w_o, moe_w_grp, moe_b_grp, moe_w_rt, moe_b_rt, moe_w1, moe_w3, moe_w2):
    nbp, tp, _ = x_prompt.shape
    nbs, ts, _ = x_sample.shape
    n_p, n_s = nbp * tp, nbs * ts
    n_all = n_p + n_s
    tm_p = min(512, tp)
    tps_p = tp // tm_p
    tm_s = n_s
    assert n_p % tm_s == 0 and n_p % CMB_TM == 0

    xp = x_prompt.reshape(n_p, D)
    xs = x_sample.reshape(n_s, D)
    mods = _ada_all(jnp.concatenate([c_prompt, c_sample], axis=0), w_ada, b_ada)

    a_caches = [(cache_k_l0, cache_v_l0), (cache_k_l3, cache_v_l3)]
    new_states = []
    for i in range(DEPTH):
        kind, j = i % 3, i // 3
        m_p = [mods[i, :nbp, n * D:(n + 1) * D].reshape(nbp, 1, D) for n in range(6)]
        m_sb = [mods[i, nbp:, n * D:(n + 1) * D].reshape(nbs, 1, D) for n in range(6)]
        m_s = [jnp.broadcast_to(m, (nbs, ts, D)).reshape(n_s, D) for m in m_sb]

        if kind == 0:
            w = a_w_qkv[j].astype(BF16)
            splits = ((0, D), (D, D + 2 * A_KV * HD))
            q_p, kv_p = _qkv(xp, m_p[0], m_p[1], w, splits, (BF16, F32), False, tm_p, tps_p)
            q_s, kv_s = _qkv(xs, m_s[0], m_s[1], w, splits, (BF16, F32), True, tm_s, 1)
            o_p = _swa_prompt(q_p, kv_p, a_sinks[j], nbp, tp)
            o_s = _swa_sample(q_s, kv_s, a_caches[j][0], a_caches[j][1], a_sinks[j], nbs, ts)
            kw = A_KV * HD
            kv_p3 = kv_p.reshape(nbp, tp, 2 * kw)[:, tp - WINDOW:]
            kv_s3 = kv_s.reshape(nbs, ts, 2 * kw)
            k_s = jnp.concatenate([a_caches[j][0], kv_s3[..., :kw].reshape(nbs, ts, A_KV, HD)], axis=1)[:, ts:]
            v_s = jnp.concatenate([a_caches[j][1], kv_s3[..., kw:].reshape(nbs, ts, A_KV, HD)], axis=1)[:, ts:]
            new_states.append((kv_p3[..., :kw].reshape(nbp, WINDOW, A_KV, HD),
                               kv_p3[..., kw:].reshape(nbp, WINDOW, A_KV, HD), k_s, v_s))
            w_o = a_w_o[j]
        elif kind == 1:
            wts = (b_mu[j], b_w_rkv[j].astype(BF16), b_w0[j].reshape(1, D), b_w1[j].astype(BF16),
                   b_w2[j].astype(BF16), b_a0[j].reshape(1, D), b_a1[j].astype(BF16), b_a2[j].astype(BF16),
                   b_g1[j].astype(BF16), b_g2[j].astype(BF16))
            zero8 = jnp.zeros((3, D), F32)
            head_w = jnp.concatenate([b_k_k[j].reshape(1, D), b_k_a[j].reshape(1, D), b_r_k[j].reshape(1, D),
                                      b_lnx_g[j].reshape(1, D), b_lnx_b[j].reshape(1, D), zero8], axis=0)
            *rk_p, hl_p = _rwkv_proj(xp, jnp.zeros((nbp, D), F32), m_p[0], m_p[1], wts, nbp, tp)
            *rk_s, hl_s = _rwkv_proj(xs, state_shift_l1, m_sb[0], m_sb[1], wts, nbs, ts)
            s0_p = jnp.zeros((nbp, D // LANES, LANES, LANES), F32)
            o_p, st_p = _rwkv_scan(*rk_p, s0_p, head_w, nbp, tp)
            o_s, st_s = _rwkv_scan(*rk_s, _state_to_blockdiag(state_wkv_l1), head_w, nbs, ts)
            new_states.append((_blockdiag_to_state(st_p), hl_p.reshape(nbp, D),
                               _blockdiag_to_state(st_s), hl_s.reshape(nbs, D)))
            w_o = b_w_o[j]
        else:
            lam_init = 0.8 - 0.6 * math.exp(-0.3 * i)
            w = c_w_qkv[j].astype(BF16)
            splits = ((0, D), (D, 2 * D), (2 * D, 3 * D))
            q_p, k_p, v_p = _qkv(xp, m_p[0], m_p[1], w, splits, (BF16, F32, F32), False, tm_p, tps_p)
            q_s, k_s, v_s = _qkv(xs, m_s[0], m_s[1], w, splits, (BF16, F32, F32), True, tm_s, 1)
            o_p = _diff_prompt(q_p, k_p, v_p, c_lam[j], c_subln_g[j], nbp, tp, lam_init)
            o_s = _diff_sample(q_s, k_s, v_s, cache_k_l2, cache_v_l2, c_lam[j], c_subln_g[j], nbs, ts, lam_init)
            nh = D // LANES
            new_states.append((k_p.reshape(nbp, tp, nh, LANES), v_p.reshape(nbp, tp, nh, LANES),
                               k_s.reshape(nbs, ts, nh, LANES), v_s.reshape(nbs, ts, nh, LANES)))
            w_o = c_w_o[j]

        wr = jnp.concatenate([moe_w_rt[i], moe_w_grp[i], jnp.zeros((D, LANES - N_EXPERTS - N_GROUPS), F32)], axis=1)
        wr_hi = wr.astype(BF16)
        wr_lo = (wr - wr_hi.astype(F32)).astype(BF16)
        wr_hl = jnp.concatenate([wr_hi, wr_lo], axis=1)
        br = jnp.concatenate([moe_b_rt[i], moe_b_grp[i], jnp.zeros((LANES - N_EXPERTS - N_GROUPS,), F32)]).reshape(1, LANES)
        lg0, lb0 = ln_g[i, 0].reshape(1, D), ln_b[i, 0].reshape(1, D)
        lg1, lb1 = ln_g[i, 1].reshape(1, D), ln_b[i, 1].reshape(1, D)
        w_o_b = w_o.astype(BF16)

        xp, hm, rt = _outln(o_p, xp, m_p[2], w_o_b, lg0, lb0, m_p[3], m_p[4], wr_hl, wr_hi, br,
                            False, tm_p, tps_p, n_all)
        xs, hm, rt = _outln(o_s, xs, m_s[2], w_o_b, lg0, lb0, m_s[3], m_s[4], wr_hl, wr_hi, br,
                            True, tm_s, 1, n_all, hm_rt=(hm, rt))

        e_flat = rt[:, :2].astype(I32).reshape(-1)
        block_expert, slot_tok, pos = _dispatch(e_flat)
        yb = _ffn(block_expert, slot_tok, hm, moe_w1[i].astype(BF16), moe_w3[i].astype(BF16), moe_w2[i].astype(BF16))
        xp = _combine(pos[:2 * n_p], yb, rt[:n_p], xp, m_p[5], lg1, lb1, False, tp // min(CMB_TM, n_p))
        xs = _combine(pos[2 * n_p:], yb, rt[n_p:], xs, m_s[5], lg1, lb1, True, 1)

    (k0p, v0p, k0s, v0s), (wkv1p, sh1p, wkv1s, sh1s), (k2p, v2p, k2s, v2s), (k3p, v3p, k3s, v3s) = new_states
    return (xp.reshape(nbp, tp, D), xs.reshape(nbs, ts, D), k0p, v0p, k0s, v0s, wkv1p, sh1p, wkv1s, sh1s,
            k2p, v2p, k2s, v2s, k3p, v3p, k3s, v3s)
```

```python
import functools
import math

import jax
import jax.numpy as jnp
from jax import lax
from jax.experimental import pallas as pl
from jax.experimental.pallas import tpu as pltpu

F32 = jnp.float32
BF16 = jnp.bfloat16
I32 = jnp.int32

D = 1024
DEPTH = 4
LANES = 128
HD = 64

A_HEADS, A_KV = 16, 4
WINDOW, CHUNK = 128, 64
N_EXPERTS, N_GROUPS, EPG, D_EXPERT = 32, 4, 8, 256
GN_EPS = 64e-5
LN_EPS = 1e-5
NEG = -1e30
ALPHA = (2 * DEPTH) ** 0.25

FFN_BLK = 256
CMB_TM = 256
VMEM_LIMIT = 56 * 1024 * 1024


def _cp(sem):
    return pltpu.CompilerParams(dimension_semantics=sem, vmem_limit_bytes=VMEM_LIMIT)


def _dot(a, b):
    return jnp.dot(a, b, preferred_element_type=F32)


def _dot_nt(a, b):
    return lax.dot_general(a, b, (((1,), (1,)), ((), ())), preferred_element_type=F32)


def _split2(x):
    hi = x.astype(BF16)
    lo = (x - hi.astype(F32)).astype(BF16)
    return hi, lo


def _split3(x):
    hi = x.astype(BF16)
    r1 = x - hi.astype(F32)
    mid = r1.astype(BF16)
    lo = (r1 - mid.astype(F32)).astype(BF16)
    return hi, mid, lo


def _dot3(a, b, dot=_dot):
    ah, al = _split2(a)
    bh, bl = _split2(b)
    m = a.shape[0]
    if m % 16:
        return dot(ah, bh) + (dot(ah, bl) + dot(al, bh))
    top = dot(jnp.concatenate([ah, al], axis=0), bh)
    return top[:m] + (top[m:] + dot(ah, bl))


def _dot_sel(a, sel_bf16):
    h, m, l = _split3(a)
    return _dot(h, sel_bf16) + (_dot(m, sel_bf16) + _dot(l, sel_bf16))


def _layer_norm(y, g, b):
    mu = jnp.mean(y, -1, keepdims=True)
    yc = y - mu
    var = jnp.mean(yc * yc, -1, keepdims=True)
    return yc * lax.rsqrt(var + LN_EPS) * g + b


def _mod_spec(per_token, tm, tiles_per_seq):
    if per_token:
        return pl.BlockSpec((tm, D), lambda i: (i, 0))
    return pl.BlockSpec((None, 1, D), lambda i: (i // tiles_per_seq, 0, 0))


def _full(shape):
    return pl.BlockSpec(shape, lambda *_: (0,) * len(shape))


def _ada_kernel(c_ref, w_ref, b_ref, o_ref):
    c = c_ref[...]
    s = c * jax.nn.sigmoid(c)
    o_ref[...] = _dot3(s, w_ref[...]) + b_ref[...]


def _ada_all(c_all, w_ada, b_ada):
    nb = c_all.shape[0]
    tn = 1536
    return pl.pallas_call(
        _ada_kernel,
        grid=(DEPTH, 6 * D // tn),
        in_specs=[pl.BlockSpec((nb, D), lambda l, j: (0, 0)),
                  pl.BlockSpec((None, D, tn), lambda l, j: (l, 0, j)),
                  pl.BlockSpec((None, 1, tn), lambda l, j: (l, 0, j))],
        out_specs=pl.BlockSpec((None, nb, tn), lambda l, j: (l, 0, j)),
        out_shape=jax.ShapeDtypeStruct((DEPTH, nb, 6 * D), F32),
        compiler_params=_cp(("arbitrary", "arbitrary")),
        name="ada",
    )(c_all, w_ada, b_ada.reshape(DEPTH, 1, 6 * D))


def _qkv_kernel(x_ref, sh_ref, sc_ref, w_ref, *o_refs, splits):
    h = (x_ref[...] * (1.0 + sc_ref[...]) + sh_ref[...]).astype(BF16)
    for o_ref, (c0, c1) in zip(o_refs, splits):
        o_ref[...] = _dot(h, w_ref[:, c0:c1]).astype(o_ref.dtype)


def _qkv(x, sh, sc, w_bf16, splits, dtypes, per_token, tm, tiles_per_seq):
    n = x.shape[0]
    nout = w_bf16.shape[1]
    mod = _mod_spec(per_token, tm, tiles_per_seq)
    return pl.pallas_call(
        functools.partial(_qkv_kernel, splits=splits),
        grid=(n // tm,),
        in_specs=[pl.BlockSpec((tm, D), lambda i: (i, 0)), mod, mod, _full((D, nout))],
        out_specs=[pl.BlockSpec((tm, c1 - c0), lambda i: (i, 0)) for c0, c1 in splits],
        out_shape=[jax.ShapeDtypeStruct((n, c1 - c0), dt) for (c0, c1), dt in zip(splits, dtypes)],
        compiler_params=_cp(("arbitrary",)),
        name="qkv",
    )(x, sh, sc, w_bf16)


def _gqa_rows(q_slab, k_slabs, v_slabs, sinks_ref, n_invalid):
    def pair(j):
        outs = [None] * 4
        slabs = [q_slab(4 * j + a) for a in range(4)]
        nq = slabs[0].shape[0]
        half = lax.broadcasted_iota(I32, (nq, LANES), 1) // HD
        aligned = [jnp.where(half == a // 2, q, jnp.zeros_like(q)) for a, q in enumerate(slabs)]
        crossed = [jnp.where(half != a // 2, q, jnp.zeros_like(q)) for a, q in enumerate(slabs)]
        qa = jnp.concatenate(aligned, axis=0) * (HD ** -0.5)
        qc = jnp.concatenate(crossed, axis=0) * (HD ** -0.5)
        s = jnp.concatenate([_dot_nt(qa, k_slabs[j][0]), _dot_nt(qc, k_slabs[j][1])], axis=0)
        yield
        if n_invalid is not None:
            col = lax.broadcasted_iota(I32, s.shape, 1)
            s = jnp.where(col >= n_invalid, s, NEG)
        sink = jnp.concatenate(
            [jnp.full((nq, 1), sinks_ref[2 * (4 * j + g % 4) + (g % 4 // 2 if g < 4 else 1 - g % 4 // 2)], F32)
             for g in range(8)], axis=0)
        m = jnp.maximum(jnp.max(s, -1, keepdims=True), sink)
        yield
        pe = jnp.exp(s - m)
        den = jnp.sum(pe, -1, keepdims=True) + jnp.exp(sink - m)
        pv = _dot(pe.astype(BF16), v_slabs[j])
        yield
        pv = pv / den
        pa = pv[:4 * nq]
        pc = pltpu.roll(pv[4 * nq:], HD, 1)
        for a in range(4):
            ra, rc = pa[a * nq:(a + 1) * nq], pc[a * nq:(a + 1) * nq]
            outs[a] = jnp.where(half == a // 2, ra, rc)
        return outs

    return [pair(j) for j in range(A_KV // 2)]


def _kv_slabs(k_f32, v_f32):
    ks, vs = [], []
    for j in range(A_KV // 2):
        kj = k_f32[:, j * LANES:(j + 1) * LANES]
        ks.append((kj.astype(BF16), pltpu.roll(kj, HD, 1).astype(BF16)))
        vs.append(v_f32[:, j * LANES:(j + 1) * LANES].astype(BF16))
    return ks, vs


def _swa_prompt_kernel(sinks_ref, q_ref, kvp_ref, kvc_ref, o_ref, *, tq):
    i = pl.program_id(1)
    kv = jnp.concatenate([kvp_ref[...], kvc_ref[...]], axis=0)
    kw = A_KV * HD
    ks, vs = _kv_slabs(kv[:, :kw], kv[:, kw:])
    gens = []
    for c in range(tq // CHUNK):
        r0, r1 = c * CHUNK, c * CHUNK + WINDOW + CHUNK
        kc = [(a[r0:r1], b[r0:r1]) for a, b in ks]
        vc = [a[r0:r1] for a in vs]
        n_bad = max(WINDOW - c * CHUNK, 0)
        n_invalid = jnp.where(i == 0, n_bad, 0) if n_bad else None
        gens += _gqa_rows(lambda p, c=c: q_ref[c * CHUNK:(c + 1) * CHUNK, p * LANES:(p + 1) * LANES],
                          kc, vc, sinks_ref, n_invalid)
    for n, outs in enumerate(_lockstep(gens)):
        c, j = n // 2, n % 2
        for a, o in enumerate(outs):
            p = 4 * j + a
            o_ref[c * CHUNK:(c + 1) * CHUNK, p * LANES:(p + 1) * LANES] = o.astype(o_ref.dtype)


def _swa_prompt(q, kv, sinks, nb, t):
    tq = min(256, t)
    tps = t // tq
    wb = tq // WINDOW
    return pl.pallas_call(
        functools.partial(_swa_prompt_kernel, tq=tq),
        grid=(nb, tps),
        in_specs=[pl.BlockSpec(memory_space=pltpu.SMEM),
                  pl.BlockSpec((tq, D), lambda b, i: (b * tps + i, 0)),
                  pl.BlockSpec((WINDOW, 2 * A_KV * HD), lambda b, i: (jnp.maximum((b * tps + i) * wb - 1, 0), 0)),
                  pl.BlockSpec((tq, 2 * A_KV * HD), lambda b, i: (b * tps + i, 0))],
        out_specs=pl.BlockSpec((tq, D), lambda b, i: (b * tps + i, 0)),
        out_shape=jax.ShapeDtypeStruct((nb * t, D), BF16),
        compiler_params=_cp(("arbitrary", "arbitrary")),
        name="swa_prompt",
    )(sinks, q, kv, kv)


def _swa_sample_kernel(sinks_ref, q_ref, ck_ref, cv_ref, kvn_ref, o_ref):
    kw = A_KV * HD
    kvn = kvn_ref[...]
    k_all = jnp.concatenate([ck_ref[...], kvn[:, :kw]], axis=0)
    v_all = jnp.concatenate([cv_ref[...], kvn[:, kw:]], axis=0)
    ks, vs = _kv_slabs(k_all, v_all)
    pairs = _lockstep(_gqa_rows(lambda p: q_ref[:, p * LANES:(p + 1) * LANES], ks, vs, sinks_ref, None))
    for p, o in enumerate(pairs[0] + pairs[1]):
        o_ref[:, p * LANES:(p + 1) * LANES] = o.astype(o_ref.dtype)


def _swa_sample(q, kv, cache_k, cache_v, sinks, nb, s):
    kw = A_KV * HD
    return pl.pallas_call(
        _swa_sample_kernel,
        grid=(nb,),
        in_specs=[pl.BlockSpec(memory_space=pltpu.SMEM),
                  pl.BlockSpec((s, D), lambda b: (b, 0)),
                  pl.BlockSpec((None, WINDOW, kw), lambda b: (b, 0, 0)),
                  pl.BlockSpec((None, WINDOW, kw), lambda b: (b, 0, 0)),
                  pl.BlockSpec((s, 2 * kw), lambda b: (b, 0))],
        out_specs=pl.BlockSpec((s, D), lambda b: (b, 0)),
        out_shape=jax.ShapeDtypeStruct((nb * s, D), BF16),
        compiler_params=_cp(("arbitrary",)),
        name="swa_sample",
    )(sinks, q, cache_k.reshape(nb, WINDOW, kw), cache_v.reshape(nb, WINDOW, kw), kv)


def _diff_lambda(lam_ref, lam_init):
    l = lam_ref[...]
    return (jnp.exp(jnp.sum(l[0:1] * l[1:2], -1, keepdims=True))
            - jnp.exp(jnp.sum(l[2:3] * l[3:4], -1, keepdims=True)) + lam_init)


def _stack_halves(q):
    half = lax.broadcasted_iota(I32, q.shape, 1) // HD
    z = jnp.zeros_like(q)
    return jnp.concatenate([jnp.where(half == 0, q, z), jnp.where(half == 1, q, z)], axis=0)


def _diff_finish(o2, n, lam_full, sg, lam_init):
    o = o2[:n] - lam_full * o2[n:]
    return o * lax.rsqrt(jnp.mean(o * o, -1, keepdims=True) + LN_EPS) * sg * (1.0 - lam_init)


def _diff_prompt_kernel(lam_ref, q_ref, k_ref, v_ref, sg_ref, o_ref, kb, vtb, *, tq, lam_init):
    i = pl.program_id(2)
    nhs = kb.shape[0]

    @pl.when(i == 0)
    def _():
        for hh in range(nhs):
            lanes = slice(hh * LANES, (hh + 1) * LANES)
            for c in range(kb.shape[1]):
                kb[hh, c] = k_ref[c * tq:(c + 1) * tq, lanes].astype(BF16)
                vtb[hh, c] = v_ref[c * tq:(c + 1) * tq, lanes].T.astype(BF16)

    qs = [_stack_halves(q_ref[:, hh * LANES:(hh + 1) * LANES]) * (HD ** -0.5) for hh in range(nhs)]

    def head_step(hh, j, carry, masked):
        m, l, acc = carry
        st = _dot_nt(kb[hh, j], qs[hh])
        yield
        if masked:
            kc = lax.broadcasted_iota(I32, st.shape, 0) // CHUNK
            qc = (lax.broadcasted_iota(I32, st.shape, 1) % tq) // CHUNK
            st = jnp.where(kc <= qc, st, NEG)
        m_new = jnp.maximum(m, jnp.max(st, 0, keepdims=True))
        yield
        p = jnp.exp(st - m_new)
        pv = _dot(vtb[hh, j], p.astype(BF16))
        yield
        a = jnp.exp(m - m_new)
        return m_new, a * l + jnp.sum(p, 0, keepdims=True), a * acc + pv

    def step(j, carries, masked):
        return tuple(_lockstep([head_step(hh, j, carries[hh], masked) for hh in range(nhs)]))

    init = (jnp.full((1, 2 * tq), NEG, F32), jnp.zeros((1, 2 * tq), F32), jnp.zeros((LANES, 2 * tq), F32))
    carries = lax.fori_loop(0, i, lambda j, c: step(j, c, False), (init,) * nhs)
    lam = _diff_lambda(lam_ref, lam_init)
    for hh, (_, l, acc) in enumerate(step(i, carries, True)):
        o2t = acc / l
        o = (o2t[:, :tq] - lam * o2t[:, tq:]).T
        o = o * lax.rsqrt(jnp.mean(o * o, -1, keepdims=True) + LN_EPS) * sg_ref[...] * (1.0 - lam_init)
        o_ref[:, hh * LANES:(hh + 1) * LANES] = o.astype(o_ref.dtype)


def _diff_prompt(q, k, v, lam, subln_g, nb, t, lam_init):
    tq = min(256, t)
    nhs = 4
    w = nhs * LANES
    q3, k3, v3 = (z.reshape(nb, t, D) for z in (q, k, v))
    return pl.pallas_call(
        functools.partial(_diff_prompt_kernel, tq=tq, lam_init=lam_init),
        grid=(nb, D // w, t // tq),
        in_specs=[_full((4, HD)),
                  pl.BlockSpec((None, tq, w), lambda b, h, i: (b, i, h)),
                  pl.BlockSpec((None, t, w), lambda b, h, i: (b, 0, h)),
                  pl.BlockSpec((None, t, w), lambda b, h, i: (b, 0, h)),
                  _full((1, LANES))],
        out_specs=pl.BlockSpec((None, tq, w), lambda b, h, i: (b, i, h)),
        out_shape=jax.ShapeDtypeStruct((nb, t, D), BF16),
        scratch_shapes=[pltpu.VMEM((nhs, t // tq, tq, LANES), BF16), pltpu.VMEM((nhs, t // tq, LANES, tq), BF16)],
        compiler_params=_cp(("arbitrary", "arbitrary", "arbitrary")),
        name="diff_prompt",
    )(lam, q3, k3, v3, subln_g.reshape(1, LANES)).reshape(nb * t, D)


def _diff_sample_kernel(lam_ref, q_ref, kn_ref, vn_ref, ck_ref, cv_ref, sg_ref, o_ref, *, lam_init):
    n = q_ref.shape[0]
    qs = _stack_halves(q_ref[...])
    s1 = _dot_nt(qs, ck_ref[...].astype(BF16)) * (HD ** -0.5)
    s2 = _dot_nt(qs, kn_ref[...].astype(BF16)) * (HD ** -0.5)
    m = jnp.maximum(jnp.max(s1, -1, keepdims=True), jnp.max(s2, -1, keepdims=True))
    p1 = jnp.exp(s1 - m)
    p2 = jnp.exp(s2 - m)
    l = jnp.sum(p1, -1, keepdims=True) + jnp.sum(p2, -1, keepdims=True)
    acc = _dot(p1.astype(BF16), cv_ref[...].astype(BF16)) + _dot(p2.astype(BF16), vn_ref[...].astype(BF16))
    o = _diff_finish(acc / l, n, _diff_lambda(lam_ref, lam_init), sg_ref[...], lam_init)
    o_ref[...] = o.astype(o_ref.dtype)


def _diff_sample(q, k, v, cache_k, cache_v, lam, subln_g, nb, s, lam_init):
    nh = D // LANES
    past = cache_k.shape[1]
    q3, k3, v3 = (z.reshape(nb, s, D) for z in (q, k, v))
    new = pl.BlockSpec((None, s, LANES), lambda b, h: (b, 0, h))
    old = pl.BlockSpec((None, past, LANES), lambda b, h: (b, 0, h))
    return pl.pallas_call(
        functools.partial(_diff_sample_kernel, lam_init=lam_init),
        grid=(nb, nh),
        in_specs=[_full((4, HD)), new, new, new, old, old, _full((1, LANES))],
        out_specs=new,
        out_shape=jax.ShapeDtypeStruct((nb, s, D), BF16),
        compiler_params=_cp(("arbitrary", "arbitrary")),
        name="diff_sample",
    )(lam, q3, k3, v3, cache_k.reshape(nb, past, D), cache_v.reshape(nb, past, D),
      subln_g.reshape(1, LANES)).reshape(nb * s, D)


def _rwkv_proj_kernel(x_ref, xp_ref, hp_ref, sh_ref, sc_ref, mu_ref, wrkv_ref, w0_ref, w1_ref, w2_ref,
                      a0_ref, a1_ref, a2_ref, g1_ref, g2_ref,
                      r_ref, k_ref, v_ref, lw_ref, a_ref, g_ref, hl_ref):
    i = pl.program_id(1)
    sc1 = 1.0 + sc_ref[...]
    sh = sh_ref[...]
    h = x_ref[...] * sc1 + sh
    prev_in_seq = xp_ref[7:8, :] * sc1 + sh
    prev = jnp.where(i == 0, hp_ref[...], prev_in_seq)
    row = lax.broadcasted_iota(I32, h.shape, 0)
    hs = jnp.where(row == 0, prev, pltpu.roll(h, 1, 0))
    xx = hs - h
    mu = mu_ref[...]

    def mix(n):
        return (h + xx * mu[n:n + 1]).astype(BF16)

    r_ref[...] = _dot(mix(0), wrkv_ref[0])
    k_ref[...] = _dot(mix(2), wrkv_ref[1])
    v_ref[...] = _dot(mix(3), wrkv_ref[2])
    z = w0_ref[...] + _dot(jnp.tanh(_dot(mix(1), w1_ref[...])).astype(BF16), w2_ref[...])
    softplus_neg = jnp.maximum(-z, 0.0) + jnp.log(1.0 + jnp.exp(-jnp.abs(z)))
    lw_ref[...] = -jnp.exp(-softplus_neg - 0.5)
    a_ref[...] = jax.nn.sigmoid(a0_ref[...] + _dot(_dot(mix(4), a1_ref[...]).astype(BF16), a2_ref[...]))
    g_ref[...] = _dot(jax.nn.sigmoid(_dot(mix(5), g1_ref[...])).astype(BF16), g2_ref[...])
    hl_ref[...] = h[h.shape[0] - 1:, :]


def _rwkv_proj(x, h_prev, sh, sc, wts, nb, t):
    tm = min(512, t)
    tps = t // tm
    x3 = x.reshape(nb, t, D)
    tok = pl.BlockSpec((None, tm, D), lambda b, i: (b, i, 0))
    per_b = pl.BlockSpec((None, 1, D), lambda b, i: (b, 0, 0))
    ins = [tok,
           pl.BlockSpec((None, 8, D), lambda b, i: (b, jnp.maximum(i * (tm // 8) - 1, 0), 0)),
           per_b, per_b, per_b] + [_full(w.shape) for w in wts]
    outs = pl.pallas_call(
        _rwkv_proj_kernel,
        grid=(nb, tps),
        in_specs=ins,
        out_specs=[tok] * 6 + [per_b],
        out_shape=[jax.ShapeDtypeStruct((nb, t, D), F32)] * 6 + [jax.ShapeDtypeStruct((nb, 1, D), F32)],
        compiler_params=_cp(("arbitrary", "arbitrary")),
        name="rwkv_proj",
    )(x3, x3, h_prev.reshape(nb, 1, D), sh, sc, *wts)
    return outs


def _stack(x):
    half = lax.broadcasted_iota(I32, x.shape, 1) // HD
    z = jnp.zeros_like(x)
    return jnp.concatenate([jnp.where(half == 0, x, z), jnp.where(half == 1, x, z)], axis=0)


def _unstack(x2):
    n = x2.shape[0] // 2
    return x2[:n] + x2[n:]


def _rwkv_prep_chunk(r, k, v, lw, a, kk_w, ka_w, rk_w, bd, tri):
    L = r.shape[0]
    n2 = 2 * L

    def seg(x):
        return _dot_sel(x, bd)

    stack = _stack
    kkr = k * kk_w
    kk = kkr / jnp.maximum(jnp.sqrt(seg(kkr * kkr)), 1e-12)
    kp = k * (1.0 + (a - 1.0) * ka_w)
    cum = _dot_sel_lhs(tri, lw)
    yield
    g_in = jnp.exp(cum)
    g_ex = jnp.exp(cum - lw)
    g_inv = jnp.exp(-cum)
    g_last = g_in[L - 1:L, :]
    at = kk * g_ex
    bt = kk * a * g_inv
    kt = kp * g_inv
    rt = r * g_in
    a_s, r_s = stack(at), stack(rt)
    ar = jnp.concatenate([a_s, r_s], axis=0)
    bk = jnp.concatenate([stack(bt), stack(kt)], axis=0)
    vs = stack(v)
    pair = _dot3(ar, bk, _dot_nt)
    yield
    row = lax.broadcasted_iota(I32, (n2, n2), 0)
    col = lax.broadcasted_iota(I32, (n2, n2), 1)
    strict = col < row
    incl = col <= row
    zero = jnp.zeros((n2, n2), F32)
    x = jnp.where(strict, -pair[:n2, :n2], zero)
    tinv = jnp.where(row == col, 1.0, 0.0).astype(F32) + x
    x = _dot3(x, x)
    kv = _dot3(jnp.concatenate([jnp.where(strict, pair[:n2, n2:], zero),
                                jnp.where(incl, pair[n2:, n2:], zero)], axis=0), vs)
    yield
    for _ in range(int(math.log2(L)) - 2):
        both = _dot3(jnp.concatenate([tinv, x], axis=0), x)
        tinv = tinv + both[:n2]
        x = both[n2:]
        yield
    tinv = tinv + _dot3(tinv, x)
    c_vk = _dot3(vs.T, stack(kt * g_last))
    bonus = seg(r * kp * rk_w) * v
    yield
    wu = _dot3(tinv, jnp.concatenate([a_s, kv[:n2]], axis=1))
    yield
    mw = _dot3(jnp.where(incl, -pair[n2:, :n2], zero), wu)
    gb = _dot3(wu.T, stack(bt * g_last))
    yield
    w2 = r_s + mw[:, :LANES]
    y0 = mw[:, LANES:] + kv[n2:]
    rs = lax.broadcasted_iota(I32, (LANES, LANES), 0)
    cs = lax.broadcasted_iota(I32, (LANES, LANES), 1)
    g_mat = jnp.where(rs == cs, g_last, 0.0) - gb[:LANES]
    c_mat = c_vk - gb[LANES:]
    return _unstack(w2), _unstack(y0), bonus, _unstack(g_mat), _unstack(c_mat)


def _lockstep(gens):
    results = [None] * len(gens)
    live = list(range(len(gens)))
    while live:
        still = []
        for n in live:
            try:
                next(gens[n])
                still.append(n)
            except StopIteration as stop:
                results[n] = stop.value
        live = still
    return results


def _dot_sel_lhs(sel_bf16, b):
    h, m, l = _split3(b)
    return _dot(sel_bf16, h) + (_dot(sel_bf16, m) + _dot(sel_bf16, l))


def _block_diag_ones():
    rr = lax.broadcasted_iota(I32, (LANES, LANES), 0)
    cc = lax.broadcasted_iota(I32, (LANES, LANES), 1)
    return jnp.where(rr // HD == cc // HD, 1.0, 0.0).astype(BF16)


def _rwkv_prep_kernel(r_ref, k_ref, v_ref, lw_ref, a_ref, hw_ref, w2_ref, y0_ref, bn_ref, gc_ref, cc_ref,
                      *, L, nchunk, nslab):
    bd = _block_diag_ones()
    rl = lax.broadcasted_iota(I32, (L, L), 0)
    cl = lax.broadcasted_iota(I32, (L, L), 1)
    tri = jnp.where(cl <= rl, 1.0, 0.0).astype(BF16)
    where, gens = [], []
    for c in range(nchunk):
        rows = slice(c * L, (c + 1) * L)
        for s in range(nslab):
            lanes = slice(s * LANES, (s + 1) * LANES)
            hw = hw_ref[:, lanes]
            where.append((rows, slice(c * HD, (c + 1) * HD), lanes))
            gens.append(_rwkv_prep_chunk(
                r_ref[rows, lanes], k_ref[rows, lanes], v_ref[rows, lanes], lw_ref[rows, lanes], a_ref[rows, lanes],
                hw[0:1], hw[1:2], hw[2:3], bd, tri))
    for (rows, srows, lanes), (w2, y0, bonus, gc, cc) in zip(where, _lockstep(gens)):
        w2_ref[rows, lanes] = w2
        y0_ref[rows, lanes] = y0
        bn_ref[rows, lanes] = bonus
        gc_ref[srows, lanes] = gc
        cc_ref[srows, lanes] = cc


def _rwkv_scan_kernel(w2_ref, y0_ref, bn_ref, g_ref, gc_ref, cc_ref, s0_ref, hw_ref, y_ref, sT_ref, s_scr, *, L):
    i = pl.program_id(1)
    tt = w2_ref.shape[0]
    nslab = D // LANES

    @pl.when(i == 0)
    def _():
        s_scr[...] = s0_ref[...]

    bd = _block_diag_ones()

    def chunk(c, carry):
        rows = pl.ds(pl.multiple_of(c * L, L), L)
        srows = pl.ds(pl.multiple_of(c * HD, HD), HD)

        def slab(s):
            lanes = slice(s * LANES, (s + 1) * LANES)
            hw = hw_ref[:, lanes]
            st = s_scr[s]
            ys = _dot3(_stack(w2_ref[rows, lanes]), st, _dot_nt)
            s_scr[s] = _dot3(st, _stack(gc_ref[srows, lanes])) + _stack(cc_ref[srows, lanes])
            yield
            y = _unstack(ys) + y0_ref[rows, lanes]
            ym = _dot_sel(y, bd) * (1.0 / HD)
            yield
            yc = y - ym
            yv = _dot_sel(yc * yc, bd) * (1.0 / HD)
            yield
            yn = yc * lax.rsqrt(yv + GN_EPS) * hw[3:4] + hw[4:5]
            y_ref[rows, lanes] = ((yn + bn_ref[rows, lanes]) * g_ref[rows, lanes]).astype(y_ref.dtype)

        _lockstep([slab(s) for s in range(nslab)])
        return carry

    lax.fori_loop(0, tt // L, chunk, 0)

    @pl.when(i == pl.num_programs(1) - 1)
    def _():
        sT_ref[...] = s_scr[...]


def _rwkv_scan(r, k, v, lw, a, g, s0_bd, head_w, nb, t):
    L = min(64, t)
    nchunk = min(2, t // L)
    nslab = 4
    tt1 = nchunk * L
    w = nslab * LANES
    tok1 = pl.BlockSpec((None, tt1, w), lambda b, sg, i: (b, i, sg))
    cmp1 = pl.BlockSpec((None, nchunk * HD, w), lambda b, sg, i: (b, i, sg))
    ns = (t // L) * HD
    w2, y0, bonus, gc, cc = pl.pallas_call(
        functools.partial(_rwkv_prep_kernel, L=L, nchunk=nchunk, nslab=nslab),
        grid=(nb, D // w, t // tt1),
        in_specs=[tok1] * 5 + [pl.BlockSpec((8, w), lambda b, sg, i: (0, sg))],
        out_specs=[tok1] * 3 + [cmp1] * 2,
        out_shape=[jax.ShapeDtypeStruct((nb, t, D), F32)] * 3 + [jax.ShapeDtypeStruct((nb, ns, D), F32)] * 2,
        compiler_params=_cp(("arbitrary", "arbitrary", "arbitrary")),
        name="rwkv_prep",
    )(r, k, v, lw, a, head_w)

    tt = min(256, t)
    tok = pl.BlockSpec((None, tt, D), lambda b, i: (b, i, 0))
    cmp = pl.BlockSpec((None, (tt // L) * HD, D), lambda b, i: (b, i, 0))
    st = pl.BlockSpec((None, D // LANES, LANES, LANES), lambda b, i: (b, 0, 0, 0))
    y, s_t = pl.pallas_call(
        functools.partial(_rwkv_scan_kernel, L=L),
        grid=(nb, t // tt),
        in_specs=[tok] * 4 + [cmp] * 2 + [st, _full((8, D))],
        out_specs=[tok, st],
        out_shape=[jax.ShapeDtypeStruct((nb, t, D), BF16),
                   jax.ShapeDtypeStruct((nb, D // LANES, LANES, LANES), F32)],
        scratch_shapes=[pltpu.VMEM((D // LANES, LANES, LANES), F32)],
        compiler_params=_cp(("arbitrary", "arbitrary")),
        name="rwkv_scan",
    )(w2, y0, bonus, g, gc, cc, s0_bd, head_w)
    return y.reshape(nb * t, D), s_t


def _state_to_blockdiag(s):
    nb = s.shape[0]
    s = s.astype(F32).reshape(nb, D // LANES, 2, HD, HD)
    z = jnp.zeros_like(s[:, :, 0])
    top = jnp.concatenate([s[:, :, 0], z], axis=-1)
    bot = jnp.concatenate([z, s[:, :, 1]], axis=-1)
    return jnp.concatenate([top, bot], axis=-2)


def _blockdiag_to_state(sb):
    nb = sb.shape[0]
    s0 = sb[:, :, :HD, :HD]
    s1 = sb[:, :, HD:, HD:]
    return jnp.stack([s0, s1], axis=2).reshape(nb, D // HD, HD, HD)


def _route(h, wr_hl_ref, wr_hi_ref, br_ref):
    hh, hl = _split2(h)
    both = _dot(hh, wr_hl_ref[...])
    lg = both[:, :LANES] + both[:, LANES:] + _dot(hl, wr_hi_ref[...]) + br_ref[...]
    lane = lax.broadcasted_iota(I32, lg.shape, 1)
    lane_f = lane.astype(F32)
    big = 1e9
    is_g = (lane >= N_EXPERTS) & (lane < N_EXPERTS + N_GROUPS)
    gl = jnp.where(is_g, lg, -jnp.inf)
    gmax = jnp.max(gl, -1, keepdims=True)
    gidx = jnp.min(jnp.where(gl == gmax, lane_f - N_EXPERTS, big), -1, keepdims=True)
    gw = 1.0 / jnp.sum(jnp.exp(gl - gmax), -1, keepdims=True)
    in_grp = (lane < N_EXPERTS) & ((lane // EPG).astype(F32) == gidx)
    el = jnp.where(in_grp, lg, -jnp.inf)
    m1 = jnp.max(el, -1, keepdims=True)
    i1 = jnp.min(jnp.where(el == m1, lane_f, big), -1, keepdims=True)
    el2 = jnp.where(lane_f == i1, -jnp.inf, el)
    m2 = jnp.max(el2, -1, keepdims=True)
    i2 = jnp.min(jnp.where(el2 == m2, lane_f, big), -1, keepdims=True)
    e2 = jnp.exp(m2 - m1)
    g1 = gw / (1.0 + e2)
    g2 = gw * e2 / (1.0 + e2)
    return jnp.where(lane == 0, i1, jnp.where(lane == 1, i2, jnp.where(lane == 2, g1, jnp.where(lane == 3, g2, 0.0))))


def _outln_kernel(*refs, aliased):
    if aliased:
        refs = refs[2:]
    (o_ref, x_ref, gt_ref, w_ref, g_ref, b_ref, shf_ref, scf_ref, wr_hl_ref, wr_hi_ref, br_ref,
     xn_ref, hm_ref, rt_ref) = refs
    op = _dot(o_ref[...], w_ref[...])
    xn = _layer_norm(ALPHA * x_ref[...] + (1.0 + gt_ref[...]) * op, g_ref[...], b_ref[...])
    xn_ref[...] = xn
    h = xn * (1.0 + scf_ref[...]) + shf_ref[...]
    _store_rows(hm_ref, h)
    rt_ref[...] = _route(h, wr_hl_ref, wr_hi_ref, br_ref)


def _outln(o, x, gt, w_o, ln_g, ln_b, shf, scf, wr_hl, wr_hi, br, per_token, tm, tiles_per_seq, n_all, hm_rt=None):
    n = x.shape[0]
    nt = n // tm
    blk0 = 0 if hm_rt is None else (n_all - n) // tm
    steps = nt if hm_rt is not None else n_all // tm
    if per_token:
        mod = pl.BlockSpec((tm, D), lambda i: (jnp.minimum(i, nt - 1), 0))
    else:
        mod = pl.BlockSpec((None, 1, D), lambda i: (jnp.minimum(i, nt - 1) // tiles_per_seq, 0, 0))
    tok = pl.BlockSpec((tm, D), lambda i: (jnp.minimum(i, nt - 1), 0))
    ins = [tok, tok, mod, _full((D, D)), _full((1, D)), _full((1, D)), mod, mod,
           _full((D, 2 * LANES)), _full((D, LANES)), _full((1, LANES))]
    args = [o, x, gt, w_o, ln_g, ln_b, shf, scf, wr_hl, wr_hi, br]
    aliases = {}
    if hm_rt is not None:
        ins = [pl.BlockSpec(memory_space=pl.ANY)] * 2 + ins
        args = list(hm_rt) + args
        aliases = {0: 1, 1: 2}
    return pl.pallas_call(
        functools.partial(_outln_kernel, aliased=hm_rt is not None),
        grid=(steps,),
        in_specs=ins,
        out_specs=[tok, pl.BlockSpec((tm * ROW_TILE, LANES), lambda i: (i + blk0, 0)),
                   pl.BlockSpec((tm, LANES), lambda i: (i + blk0, 0))],
        out_shape=[jax.ShapeDtypeStruct((n, D), F32), jax.ShapeDtypeStruct((n_all * ROW_TILE, LANES), F32),
                   jax.ShapeDtypeStruct((n_all, LANES), F32)],
        input_output_aliases=aliases,
        compiler_params=_cp(("arbitrary",)),
        name="outln",
    )(*args)


ROW_TILE = D // LANES


def _store_rows(ref2, x):
    n = x.shape[0]
    for c in range(ROW_TILE):
        ref2[pl.ds(c, n, stride=ROW_TILE), :] = x[:, c * LANES:(c + 1) * LANES]


def _load_rows(ref2):
    n = ref2.shape[0] // ROW_TILE
    return jnp.concatenate([ref2[pl.ds(c, n, stride=ROW_TILE), :] for c in range(ROW_TILE)], axis=1)


def _gather_rows(idx_ref, n, src_hbm, dst, sem, stride=1, offset=0):
    for r in range(n):
        t = pl.multiple_of(idx_ref[0, r * stride + offset] * ROW_TILE, ROW_TILE)
        pltpu.make_async_copy(src_hbm.at[pl.ds(t, ROW_TILE)], dst.at[pl.ds(r * ROW_TILE, ROW_TILE)],
                              sem).start(priority=r % 2)


def _wait_rows(src_hbm, dst, sem):
    n = dst.shape[0]
    pltpu.make_async_copy(src_hbm.at[pl.ds(0, n)], dst, sem).wait()


def _ffn_kernel(be_ref, tok_ref, tokn_ref, x_hbm, w1_ref, w3_ref, w2_ref, y_ref, buf, sem):
    del be_ref
    i = pl.program_id(0)
    last = pl.num_programs(0) - 1
    slot = i % 2

    @pl.when(i == 0)
    def _():
        _gather_rows(tok_ref, FFN_BLK, x_hbm, buf.at[0], sem.at[0])

    _wait_rows(x_hbm, buf.at[slot], sem.at[slot])
    _gather_rows(tokn_ref, FFN_BLK, x_hbm, buf.at[1 - slot], sem.at[1 - slot])

    xb = _load_rows(buf.at[slot]).astype(BF16)
    a = _dot(xb, w1_ref[...])
    hdn = (a * jax.nn.sigmoid(a)) * _dot(xb, w3_ref[...])
    _store_rows(y_ref, _dot(hdn.astype(BF16), w2_ref[...]))

    @pl.when(i == last)
    def _():
        _wait_rows(x_hbm, buf.at[1 - slot], sem.at[1 - slot])


def _ffn(block_expert, slot_tok, hm, w1, w3, w2):
    nblk = block_expert.shape[0]
    tok3 = slot_tok.reshape(nblk, 1, FFN_BLK)
    grid_spec = pltpu.PrefetchScalarGridSpec(
        num_scalar_prefetch=1,
        grid=(nblk,),
        in_specs=[pl.BlockSpec((None, 1, FFN_BLK), lambda i, be: (i, 0, 0), memory_space=pltpu.SMEM),
                  pl.BlockSpec((None, 1, FFN_BLK), lambda i, be: (jnp.minimum(i + 1, nblk - 1), 0, 0),
                               memory_space=pltpu.SMEM),
                  pl.BlockSpec(memory_space=pl.ANY),
                  pl.BlockSpec((None, D, D_EXPERT), lambda i, be: (be[i], 0, 0)),
                  pl.BlockSpec((None, D, D_EXPERT), lambda i, be: (be[i], 0, 0)),
                  pl.BlockSpec((None, D_EXPERT, D), lambda i, be: (be[i], 0, 0))],
        out_specs=pl.BlockSpec((FFN_BLK * ROW_TILE, LANES), lambda i, be: (i, 0)),
        scratch_shapes=[pltpu.VMEM((2, FFN_BLK * ROW_TILE, LANES), F32), pltpu.SemaphoreType.DMA((2,))],
    )
    return pl.pallas_call(
        _ffn_kernel,
        grid_spec=grid_spec,
        out_shape=jax.ShapeDtypeStruct((nblk * FFN_BLK * ROW_TILE, LANES), F32),
        compiler_params=_cp(("arbitrary",)),
        name="moe_ffn",
    )(block_expert, tok3, tok3, hm, w1, w3, w2)


def _combine_kernel(pos_ref, posn_ref, yb_hbm, rt_ref, x_ref, gt_ref, g_ref, b_ref, o_ref, buf, sem):
    i = pl.program_id(0)
    last = pl.num_programs(0) - 1
    slot = i % 2
    tm = x_ref.shape[0]

    def issue(idx_ref, s):
        _gather_rows(idx_ref, tm, yb_hbm, buf.at[s, 0], sem.at[s], stride=2, offset=0)
        _gather_rows(idx_ref, tm, yb_hbm, buf.at[s, 1], sem.at[s], stride=2, offset=1)

    def drain(s):
        for kk in range(2):
            _wait_rows(yb_hbm, buf.at[s, kk], sem.at[s])

    @pl.when(i == 0)
    def _():
        issue(pos_ref, 0)

    drain(slot)
    issue(posn_ref, 1 - slot)

    rt = rt_ref[...]
    f = rt[:, 2:3] * _load_rows(buf.at[slot, 0]) + rt[:, 3:4] * _load_rows(buf.at[slot, 1])
    o_ref[...] = _layer_norm(ALPHA * x_ref[...] + (1.0 + gt_ref[...]) * f, g_ref[...], b_ref[...])

    @pl.when(i == last)
    def _():
        drain(1 - slot)


def _combine(pos, yb, rt, x, gt, ln_g, ln_b, per_token, tiles_per_seq):
    n = x.shape[0]
    tm = min(CMB_TM, n)
    nt = n // tm
    pos3 = pos.reshape(nt, 1, 2 * tm)
    mod = _mod_spec(per_token, tm, tiles_per_seq)
    tok = pl.BlockSpec((tm, D), lambda i: (i, 0))
    return pl.pallas_call(
        _combine_kernel,
        grid=(nt,),
        in_specs=[pl.BlockSpec((None, 1, 2 * tm), lambda i: (i, 0, 0), memory_space=pltpu.SMEM),
                  pl.BlockSpec((None, 1, 2 * tm), lambda i: (jnp.minimum(i + 1, nt - 1), 0, 0), memory_space=pltpu.SMEM),
                  pl.BlockSpec(memory_space=pl.ANY),
                  pl.BlockSpec((tm, LANES), lambda i: (i, 0)),
                  tok, mod, _full((1, D)), _full((1, D))],
        out_specs=tok,
        out_shape=jax.ShapeDtypeStruct((n, D), F32),
        scratch_shapes=[pltpu.VMEM((2, 2, tm * ROW_TILE, LANES), F32), pltpu.SemaphoreType.DMA((2,))],
        compiler_params=_cp(("arbitrary",)),
        name="moe_combine",
    )(pos3, pos3, yb, rt, x, gt, ln_g, ln_b)


def _dispatch(e_flat):
    n_assign = e_flat.shape[0]
    ids = jnp.arange(n_assign, dtype=I32)
    experts = jnp.arange(N_EXPERTS, dtype=I32)

    def lookup(table, e):
        return jnp.sum(jnp.where(e[:, None] == experts[None, :], table[None, :], 0), axis=1)

    e_sorted, order = lax.sort((e_flat, ids), num_keys=1, is_stable=True)
    counts = jnp.sum((e_flat[:, None] == experts[None, :]).astype(I32), axis=0)
    start = jnp.cumsum(counts) - counts
    padded = (counts + FFN_BLK - 1) // FFN_BLK * FFN_BLK
    pend = jnp.cumsum(padded)
    pstart = pend - padded
    dest = ids + lookup(pstart - start, e_sorted)
    _, pos = lax.sort((order, dest), num_keys=1)
    n_blocks = -(-(n_assign + N_EXPERTS * (FFN_BLK - 1)) // FFN_BLK)
    blk_start = jnp.arange(n_blocks, dtype=I32) * FFN_BLK
    block_expert = jnp.minimum(jnp.sum((pend[None, :] <= blk_start[:, None]).astype(I32), axis=1), N_EXPERTS - 1)
    e_slot = jnp.repeat(block_expert, FFN_BLK)
    rank = jnp.arange(n_blocks * FFN_BLK, dtype=I32) - lookup(pstart, e_slot)
    valid = rank < lookup(counts, e_slot)
    src = jnp.where(valid, rank + lookup(start, e_slot), 0)
    slot_tok = jnp.where(valid, order[src] // 2, 0)
    return block_expert, slot_tok, pos


def kernel(x_prompt, x_sample, c_prompt, c_sample, cache_k_l0, cache_v_l0, state_wkv_l1, state_shift_l1, cache_k_l2, cache_v_l2, cache_k_l3, cache_v_l3, w_ada, b_ada, ln_g, ln_b, a_w_qkv, a_sinks, a_w_o, b_mu, b_w_rkv, b_w0, b_w1, b_w2, b_a0, b_a1, b_a2, b_g1, b_g2, b_k_k, b_k_a, b_r_k, b_lnx_g, b_lnx_b, b_w_o, c_w_qkv, c_lam, c_subln_g, c_w_o, moe_w_grp, moe_b_grp, moe_w_rt, moe_b_rt, moe_w1, moe_w3, moe_w2):
    nbp, tp, _ = x_prompt.shape
    nbs, ts, _ = x_sample.shape
    n_p, n_s = nbp * tp, nbs * ts
    n_all = n_p + n_s
    tm_p = min(512, tp)
    tps_p = tp // tm_p
    tm_s = n_s
    assert n_p % tm_s == 0 and n_p % CMB_TM == 0

    xp = x_prompt.reshape(n_p, D)
    xs = x_sample.reshape(n_s, D)
    mods = _ada_all(jnp.concatenate([c_prompt, c_sample], axis=0), w_ada, b_ada)

    a_caches = [(cache_k_l0, cache_v_l0), (cache_k_l3, cache_v_l3)]
    new_states = []
    for i in range(DEPTH):
        kind, j = i % 3, i // 3
        m_p = [mods[i, :nbp, n * D:(n + 1) * D].reshape(nbp, 1, D) for n in range(6)]
        m_sb = [mods[i, nbp:, n * D:(n + 1) * D].reshape(nbs, 1, D) for n in range(6)]
        m_s = [jnp.broadcast_to(m, (nbs, ts, D)).reshape(n_s, D) for m in m_sb]

        if kind == 0:
            w = a_w_qkv[j].astype(BF16)
            splits = ((0, D), (D, D + 2 * A_KV * HD))
            q_p, kv_p = _qkv(xp, m_p[0], m_p[1], w, splits, (BF16, F32), False, tm_p, tps_p)
            q_s, kv_s = _qkv(xs, m_s[0], m_s[1], w, splits, (BF16, F32), True, tm_s, 1)
            o_p = _swa_prompt(q_p, kv_p, a_sinks[j], nbp, tp)
            o_s = _swa_sample(q_s, kv_s, a_caches[j][0], a_caches[j][1], a_sinks[j], nbs, ts)
            kw = A_KV * HD
            kv_p3 = kv_p.reshape(nbp, tp, 2 * kw)[:, tp - WINDOW:]
            kv_s3 = kv_s.reshape(nbs, ts, 2 * kw)
            k_s = jnp.concatenate([a_caches[j][0], kv_s3[..., :kw].reshape(nbs, ts, A_KV, HD)], axis=1)[:, ts:]
            v_s = jnp.concatenate([a_caches[j][1], kv_s3[..., kw:].reshape(nbs, ts, A_KV, HD)], axis=1)[:, ts:]
            new_states.append((kv_p3[..., :kw].reshape(nbp, WINDOW, A_KV, HD),
                               kv_p3[..., kw:].reshape(nbp, WINDOW, A_KV, HD), k_s, v_s))
            w_o = a_w_o[j]
        elif kind == 1:
            wts = (b_mu[j], b_w_rkv[j].astype(BF16), b_w0[j].reshape(1, D), b_w1[j].astype(BF16),
                   b_w2[j].astype(BF16), b_a0[j].reshape(1, D), b_a1[j].astype(BF16), b_a2[j].astype(BF16),
                   b_g1[j].astype(BF16), b_g2[j].astype(BF16))
            zero8 = jnp.zeros((3, D), F32)
            head_w = jnp.concatenate([b_k_k[j].reshape(1, D), b_k_a[j].reshape(1, D), b_r_k[j].reshape(1, D),
                                      b_lnx_g[j].reshape(1, D), b_lnx_b[j].reshape(1, D), zero8], axis=0)
            *rk_p, hl_p = _rwkv_proj(xp, jnp.zeros((nbp, D), F32), m_p[0], m_p[1], wts, nbp, tp)
            *rk_s, hl_s = _rwkv_proj(xs, state_shift_l1, m_sb[0], m_sb[1], wts, nbs, ts)
            s0_p = jnp.zeros((nbp, D // LANES, LANES, LANES), F32)
            o_p, st_p = _rwkv_scan(*rk_p, s0_p, head_w, nbp, tp)
            o_s, st_s = _rwkv_scan(*rk_s, _state_to_blockdiag(state_wkv_l1), head_w, nbs, ts)
            new_states.append((_blockdiag_to_state(st_p), hl_p.reshape(nbp, D),
                               _blockdiag_to_state(st_s), hl_s.reshape(nbs, D)))
            w_o = b_w_o[j]
        else:
            lam_init = 0.8 - 0.6 * math.exp(-0.3 * i)
            w = c_w_qkv[j].astype(BF16)
            splits = ((0, D), (D, 2 * D), (2 * D, 3 * D))
            q_p, k_p, v_p = _qkv(xp, m_p[0], m_p[1], w, splits, (BF16, F32, F32), False, tm_p, tps_p)
            q_s, k_s, v_s = _qkv(xs, m_s[0], m_s[1], w, splits, (BF16, F32, F32), True, tm_s, 1)
            o_p = _diff_prompt(q_p, k_p, v_p, c_lam[j], c_subln_g[j], nbp, tp, lam_init)
            o_s = _diff_sample(q_s, k_s, v_s, cache_k_l2, cache_v_l2, c_lam[j], c_subln_g[j], nbs, ts, lam_init)
            nh = D // LANES
            new_states.append((k_p.reshape(nbp, tp, nh, LANES), v_p.reshape(nbp, tp, nh, LANES),
                               k_s.reshape(nbs, ts, nh, LANES), v_s.reshape(nbs, ts, nh, LANES)))
            w_o = c_w_o[j]

        wr = jnp.concatenate([moe_w_rt[i], moe_w_grp[i], jnp.zeros((D, LANES - N_EXPERTS - N_GROUPS), F32)], axis=1)
        wr_hi = wr.astype(BF16)
        wr_lo = (wr - wr_hi.astype(F32)).astype(BF16)
        wr_hl = jnp.concatenate([wr_hi, wr_lo], axis=1)
        br = jnp.concatenate([moe_b_rt[i], moe_b_grp[i], jnp.zeros((LANES - N_EXPERTS - N_GROUPS,), F32)]).reshape(1, LANES)
        lg0, lb0 = ln_g[i, 0].reshape(1, D), ln_b[i, 0].reshape(1, D)
        lg1, lb1 = ln_g[i, 1].reshape(1, D), ln_b[i, 1].reshape(1, D)
        w_o_b = w_o.astype(BF16)

        xp, hm, rt = _outln(o_p, xp, m_p[2], w_o_b, lg0, lb0, m_p[3], m_p[4], wr_hl, wr_hi, br,
                            False, tm_p, tps_p, n_all)
        xs, hm, rt = _outln(o_s, xs, m_s[2], w_o_b, lg0, lb0, m_s[3], m_s[4], wr_hl, wr_hi, br,
                            True, tm_s, 1, n_all, hm_rt=(hm, rt))

        e_flat = rt[:, :2].astype(I32).reshape(-1)
        block_expert, slot_tok, pos = _dispatch(e_flat)
        yb = _ffn(block_expert, slot_tok, hm, moe_w1[i].astype(BF16), moe_w3[i].astype(BF16), moe_w2[i].astype(BF16))
        xp = _combine(pos[:2 * n_p], yb, rt[:n_p], xp, m_p[5], lg1, lb1, False, tp // min(CMB_TM, n_p))
        xs = _combine(pos[2 * n_p:], yb, rt[n_p:], xs, m_s[5], lg1, lb1, True, 1)

    (k0p, v0p, k0s, v0s), (wkv1p, sh1p, wkv1s, sh1s), (k2p, v2p, k2s, v2s), (k3p, v3p, k3s, v3s) = new_states
    return (xp.reshape(nbp, tp, D), xs.reshape(nbs, ts, D), k0p, v0p, k0s, v0s, wkv1p, sh1p, wkv1s, sh1s,
            k2p, v2p, k2s, v2s, k3p, v3p, k3s, v3s)
```
